```python
import jax, jax.numpy as jnp
from jax import lax
import numpy as np

D_MODEL = 2048
BATCH = 4
SEQ = 8192
DEPTH = 2
DEC_BATCH = 32
DEC_SEQ = 16
PAST_LEN = 1024

CHUNK = 64
HA = 8
DN = 128
DR = 64
DVA = 128
Q_LORA = 512
KV_LORA = 256
Q_BLOCK = 128
HB = 8
DHB = 128
BAND_PREV = 8
REL_CLIP = 128
HR = 8
DKR = 64
DVR = 128
D_FF = 5632
CONV_W = 3
ROPE_THETA = 10000.0
ALPHA = (2 * DEPTH) ** 0.25
BETA = (8 * DEPTH) ** -0.25
EPS = 1e-5
SPLITS = (Q_LORA, KV_LORA, DR, HB * DHB, HB * DHB, HB * DHB,
          HR * DKR, HR * DKR, HR * DVR, HR * DVR, D_MODEL, D_MODEL, D_MODEL)
N_IN = sum(SPLITS)
SPLIT_POINTS = tuple(np.cumsum(SPLITS)[:-1].tolist())

kernel_name = 'hybrid_streaming_encoder_step'


def _layer_norm(x, g, b):
    xf = x.astype(jnp.float32)
    mu = jnp.mean(xf, -1, keepdims=True)
    var = jnp.mean(jnp.square(xf - mu), -1, keepdims=True)
    y = (xf - mu) * lax.rsqrt(var + EPS) * g.astype(jnp.float32) + b.astype(jnp.float32)
    return y.astype(x.dtype)


def _rms_norm(x, g):
    xf = x.astype(jnp.float32)
    y = xf * lax.rsqrt(jnp.mean(xf * xf, -1, keepdims=True) + EPS) * g.astype(jnp.float32)
    return y.astype(x.dtype)


def _rope(x, pos):
    half = x.shape[-1] // 2
    inv = ROPE_THETA ** (-jnp.arange(half, dtype=jnp.float32) / half)
    ang = pos.astype(jnp.float32)[:, None] * inv[None, :]
    shape = (1, x.shape[1]) + (1,) * (x.ndim - 3) + (half,)
    cos, sin = jnp.cos(ang).reshape(shape), jnp.sin(ang).reshape(shape)
    xf = x.astype(jnp.float32)
    x1, x2 = xf[..., :half], xf[..., half:]
    return jnp.concatenate([x1 * cos - x2 * sin, x2 * cos + x1 * sin], -1).astype(x.dtype)


def _mla_attend(q_nope, q_rope, k_nope, k_rope, v, q_pos, k_pos):
    s = (jnp.einsum('bqhd,bkhd->bhqk', q_nope, k_nope)
         + jnp.einsum('bqhr,bkr->bhqk', q_rope, k_rope)).astype(jnp.float32) * (DN + DR) ** -0.5
    allowed = (k_pos[None, :] // CHUNK) <= (q_pos[:, None] // CHUNK)
    s = jnp.where(allowed[None, None], s, -jnp.inf)
    p = jax.nn.softmax(s, axis=-1).astype(v.dtype)
    return jnp.einsum('bhqk,bkhe->bqhe', p, v)


def _mla_block_sweep(q_nope, q_rope, k_nope, k_rope, v, pos):
    B, T = q_nope.shape[:2]
    nb = T // Q_BLOCK

    def blk(a):
        qn, qr, qp = a
        return _mla_attend(qn, qr, k_nope, k_rope, v, qp, pos)

    o = lax.map(blk, (q_nope.reshape(B, nb, Q_BLOCK, HA, DN).swapaxes(0, 1),
                      q_rope.reshape(B, nb, Q_BLOCK, HA, DR).swapaxes(0, 1),
                      pos.reshape(nb, Q_BLOCK)))
    return o.swapaxes(0, 1).reshape(B, T, HA, DVA)


def _band_attend(q, k, v, q_pos, k_pos, rel_bias):
    s = jnp.einsum('bqhd,bkhd->bhqk', q, k).astype(jnp.float32) * DHB ** -0.5
    rel = jnp.clip(q_pos[:, None] - k_pos[None, :], -REL_CLIP, REL_CLIP) + REL_CLIP
    s = s + rel_bias[:, rel].astype(jnp.float32)[None]
    s = jnp.where((k_pos >= 0)[None, None, None, :], s, -jnp.inf)
    p = jax.nn.softmax(s, axis=-1).astype(v.dtype)
    return jnp.einsum('bhqk,bkhe->bqhe', p, v)


def _band_prompt(q, k, v, rel_bias):
    B, T = q.shape[:2]
    nc = T // CHUNK
    pad = BAND_PREV * CHUNK
    kp = jnp.pad(k, ((0, 0), (pad, 0), (0, 0), (0, 0)))
    vp = jnp.pad(v, ((0, 0), (pad, 0), (0, 0), (0, 0)))

    def one(a):
        qc, n = a
        start = n * CHUNK
        kb = lax.dynamic_slice_in_dim(kp, start, pad + CHUNK, axis=1)
        vb = lax.dynamic_slice_in_dim(vp, start, pad + CHUNK, axis=1)
        q_pos = start + jnp.arange(CHUNK, dtype=jnp.int32)
        k_pos = start - pad + jnp.arange(pad + CHUNK, dtype=jnp.int32)
        return _band_attend(qc, kb, vb, q_pos, k_pos, rel_bias)

    o = lax.map(one, (q.reshape(B, nc, CHUNK, HB, DHB).swapaxes(0, 1),
                      jnp.arange(nc, dtype=jnp.int32)))
    return o.swapaxes(0, 1).reshape(B, T, HB, DHB)


def _ret_log_decay():
    return jnp.log1p(-jnp.exp2(-5.0 - jnp.arange(HR, dtype=jnp.float32)))


def _ret_chunk(S, q, k, v, lg):
    L = q.shape[1]
    idx = jnp.arange(L, dtype=jnp.float32)
    diff = idx[:, None] - idx[None, :]
    dmask = jnp.where(diff[None] >= 0, jnp.exp(jnp.maximum(diff, 0.0)[None] * lg[:, None, None]), 0.0)
    inner = jnp.einsum('bnhd,bmhd->bhnm', q, k) * dmask[None]
    o = jnp.einsum('bhnm,bmhe->bnhe', inner, v)
    q_dec = q * jnp.exp((idx[:, None] + 1.0) * lg[None, :])[None, :, :, None]
    o = o + jnp.einsum('bnhd,bhde->bnhe', q_dec, S)
    k_dec = k * jnp.exp((L - 1.0 - idx)[:, None] * lg[None, :])[None, :, :, None]
    S_new = jnp.exp(L * lg)[None, :, None, None] * S + jnp.einsum('bmhd,bmhe->bhde', k_dec, v)
    return S_new, o


def _ret_prompt(q, k, v, lg):
    B, T = q.shape[:2]
    nc = T // CHUNK

    def to_c(a):
        return a.reshape(B, nc, CHUNK, HR, a.shape[-1]).swapaxes(0, 1)

    S0 = jnp.zeros((B, HR, DKR, DVR), jnp.float32)
    S, o = lax.scan(lambda S, a: _ret_chunk(S, a[0], a[1], a[2], lg), S0, (to_c(q), to_c(k), to_c(v)))
    return S, o.swapaxes(0, 1).reshape(B, T, HR, DVR)


def _head_norm(o, g):
    B, T = o.shape[:2]
    mu = jnp.mean(o, -1, keepdims=True)
    var = jnp.mean(jnp.square(o - mu), -1, keepdims=True)
    return ((o - mu) * lax.rsqrt(var + EPS)).reshape(B, T, HR * DVR) * g.astype(jnp.float32)


def _conv_ffn(h, prev, w_a, w_b, cw, cb, w_down):
    T = h.shape[1]
    a = h @ w_a
    b = h @ w_b
    ap = jnp.concatenate([prev.astype(a.dtype), a], axis=1)
    conv = cb + ap[:, 0:T] * cw[0] + ap[:, 1:T + 1] * cw[1] + ap[:, 2:T + 2] * cw[2]
    y = (jax.nn.gelu(conv, approximate=False) * b) @ w_down
    return y, ap[:, T:]


def _layer(x, c, pos, cache, prm):
    (w_ada, b_ada, w_in, g_q, g_kv, w_uq, w_ukv, rel_bias, g_rn, w_pa, w_pb, w_pc,
     w_o, ln1_g, ln1_b, w_fa, w_fb, cw, cb, w_fd, ln2_g, ln2_b) = prm
    B, T, _ = x.shape
    ada = jax.nn.silu(c) @ w_ada + b_ada
    sh1, sc1, gt1, sh2, sc2, gt2 = [a[:, None, :] for a in jnp.split(ada, 6, axis=-1)]
    h = x * (1.0 + sc1) + sh1
    (cq, ckv_raw, kr_raw, qb, kb, vb, qr, kr, vr, gr, g_a, g_b, g_c) = jnp.split(h @ w_in, SPLIT_POINTS, axis=-1)

    qa = (_rms_norm(cq, g_q) @ w_uq).reshape(B, T, HA, DN + DR)
    q_nope, q_rope = qa[..., :DN], _rope(qa[..., DN:], pos)
    ckv = _rms_norm(ckv_raw, g_kv)
    krope = _rope(kr_raw, pos)
    if cache is None:
        ckv_all, kr_all, k_pos = ckv, krope, pos
    else:
        ckv_all = jnp.concatenate([cache[0].astype(ckv.dtype), ckv], axis=1)
        kr_all = jnp.concatenate([cache[1].astype(krope.dtype), krope], axis=1)
        k_pos = jnp.arange(ckv_all.shape[1], dtype=jnp.int32)
    kv = (ckv_all @ w_ukv).reshape(B, ckv_all.shape[1], HA, DN + DVA)
    k_nope, va = kv[..., :DN], kv[..., DN:]
    if cache is None:
        oa = _mla_block_sweep(q_nope, q_rope, k_nope, kr_all, va, pos)
    else:
        oa = _mla_attend(q_nope, q_rope, k_nope, kr_all, va, pos, k_pos)
    oa = oa.reshape(B, T, HA * DVA)

    qb = qb.reshape(B, T, HB, DHB)
    kb = kb.reshape(B, T, HB, DHB)
    vb = vb.reshape(B, T, HB, DHB)
    if cache is None:
        ob = _band_prompt(qb, kb, vb, rel_bias)
        keep = min(BAND_PREV * CHUNK, T)
        bk_new, bv_new = kb[:, T - keep:], vb[:, T - keep:]
    else:
        kc = cache[2].astype(kb.dtype)
        vc = cache[3].astype(vb.dtype)
        nk = kc.shape[1]
        kb_pos = pos[0] - nk + jnp.arange(nk + T, dtype=jnp.int32)
        ob = _band_attend(qb, jnp.concatenate([kc, kb], 1), jnp.concatenate([vc, vb], 1), pos, kb_pos, rel_bias)
        bk_new, bv_new = kb, vb
    ob = ob.reshape(B, T, HB * DHB)

    lg = _ret_log_decay()
    q_r = _rope(qr.reshape(B, T, HR, DKR), pos).astype(jnp.float32) * DKR ** -0.5
    k_r = _rope(kr.reshape(B, T, HR, DKR), pos).astype(jnp.float32)
    v_r = vr.reshape(B, T, HR, DVR).astype(jnp.float32)
    if cache is None:
        S_new, o_r = _ret_prompt(q_r, k_r, v_r, lg)
    else:
        S_new, o_r = _ret_chunk(cache[4].astype(jnp.float32), q_r, k_r, v_r, lg)
    oc = (jax.nn.silu(gr.astype(jnp.float32)) * _head_norm(o_r, g_rn)).astype(x.dtype)

    merged = (jax.nn.sigmoid(g_a) * (oa @ w_pa) + jax.nn.sigmoid(g_b) * (ob @ w_pb)
              + jax.nn.sigmoid(g_c) * (oc @ w_pc))
    x = _layer_norm(ALPHA * x + (1.0 + gt1) * (merged @ w_o), ln1_g, ln1_b)

    h2 = x * (1.0 + sc2) + sh2
    prev = jnp.zeros((B, CONV_W - 1, D_FF), x.dtype) if cache is None else cache[5]
    y, conv_new = _conv_ffn(h2, prev, w_fa, w_fb, cw, cb, w_fd)
    x = _layer_norm(ALPHA * x + (1.0 + gt2) * y, ln2_g, ln2_b)
    return x, (ckv, krope, bk_new, bv_new, S_new, conv_new)


def setup_inputs(seed: int = 0) -> dict:
    key = jax.random.key(seed)
    ks = iter(jax.random.split(key, 48))
    f32 = jnp.float32

    def nrm(shape, scale):
        return jax.random.normal(next(ks), shape, f32) * scale

    band_cache = min(BAND_PREV * CHUNK, PAST_LEN)
    L = DEPTH
    return {
        'x_prompt': nrm((BATCH, SEQ, D_MODEL), 1.0),
        'x_sample': nrm((DEC_BATCH, DEC_SEQ, D_MODEL), 1.0),
        'c_prompt': nrm((BATCH, D_MODEL), 1.0),
        'c_sample': nrm((DEC_BATCH, D_MODEL), 1.0),
        'cache_mla_ckv': nrm((L, DEC_BATCH, PAST_LEN, KV_LORA), 1.0),
        'cache_mla_krope': nrm((L, DEC_BATCH, PAST_LEN, DR), 1.0),
        'cache_band_k': nrm((L, DEC_BATCH, band_cache, HB, DHB), 1.0),
        'cache_band_v': nrm((L, DEC_BATCH, band_cache, HB, DHB), 1.0),
        'state_ret': nrm((L, DEC_BATCH, HR, DKR, DVR), 1.0),
        'state_conv': nrm((L, DEC_BATCH, CONV_W - 1, D_FF), 1.0),
        'w_ada': nrm((L, D_MODEL, 6 * D_MODEL), 0.1 * D_MODEL ** -0.5),
        'b_ada': nrm((L, 6 * D_MODEL), 0.01),
        'w_in': nrm((L, D_MODEL, N_IN), D_MODEL ** -0.5),
        'g_q_lora': 1.0 + nrm((L, Q_LORA), 0.1),
        'g_kv_lora': 1.0 + nrm((L, KV_LORA), 0.1),
        'w_uq': nrm((L, Q_LORA, HA * (DN + DR)), Q_LORA ** -0.5),
        'w_ukv': nrm((L, KV_LORA, HA * (DN + DVA)), KV_LORA ** -0.5),
        'rel_bias': nrm((L, HB, 2 * REL_CLIP + 1), 0.2),
        'g_ret_norm': 1.0 + nrm((L, HR * DVR), 0.1),
        'w_branch_a': nrm((L, HA * DVA, D_MODEL), (HA * DVA) ** -0.5),
        'w_branch_b': nrm((L, HB * DHB, D_MODEL), (HB * DHB) ** -0.5),
        'w_branch_c': nrm((L, HR * DVR, D_MODEL), (HR * DVR) ** -0.5),
        'w_o': nrm((L, D_MODEL, D_MODEL), BETA * D_MODEL ** -0.5),
        'ln1_g': 1.0 + nrm((L, D_MODEL), 0.1),
        'ln1_b': nrm((L, D_MODEL), 0.02),
        'w_ff_a': nrm((L, D_MODEL, D_FF), D_MODEL ** -0.5),
        'w_ff_b': nrm((L, D_MODEL, D_FF), D_MODEL ** -0.5),
        'conv_w': nrm((L, CONV_W, D_FF), CONV_W ** -0.5),
        'conv_b': nrm((L, D_FF), 0.02),
        'w_ff_down': nrm((L, D_FF, D_MODEL), BETA * D_FF ** -0.5),
        'ln2_g': 1.0 + nrm((L, D_MODEL), 0.1),
        'ln2_b': nrm((L, D_MODEL), 0.02),
    }


def reference(x_prompt, x_sample, c_prompt, c_sample, cache_mla_ckv, cache_mla_krope,
              cache_band_k, cache_band_v, state_ret, state_conv, w_ada, b_ada, w_in,
              g_q_lora, g_kv_lora, w_uq, w_ukv, rel_bias, g_ret_norm, w_branch_a,
              w_branch_b, w_branch_c, w_o, ln1_g, ln1_b, w_ff_a, w_ff_b, conv_w, conv_b,
              w_ff_down, ln2_g, ln2_b):
    past = cache_mla_ckv.shape[2]
    pos_p = jnp.arange(x_prompt.shape[1], dtype=jnp.int32)
    pos_s = past + jnp.arange(x_sample.shape[1], dtype=jnp.int32)
    y_prompt, y_sample = x_prompt, x_sample
    new_p, new_s = [], []
    for l in range(DEPTH):
        prm = (w_ada[l], b_ada[l], w_in[l], g_q_lora[l], g_kv_lora[l], w_uq[l], w_ukv[l],
               rel_bias[l], g_ret_norm[l], w_branch_a[l], w_branch_b[l], w_branch_c[l],
               w_o[l], ln1_g[l], ln1_b[l], w_ff_a[l], w_ff_b[l], conv_w[l], conv_b[l],
               w_ff_down[l], ln2_g[l], ln2_b[l])
        y_prompt, st_p = _layer(y_prompt, c_prompt, pos_p, None, prm)
        cache_l = (cache_mla_ckv[l], cache_mla_krope[l], cache_band_k[l], cache_band_v[l],
                   state_ret[l], state_conv[l])
        y_sample, st_s = _layer(y_sample, c_sample, pos_s, cache_l, prm)
        new_p.append(st_p)
        new_s.append(st_s)
    mla_ckv_p, mla_kr_p, band_k_p, band_v_p, ret_p, conv_p = [jnp.stack(z) for z in zip(*new_p)]
    mla_ckv_s, mla_kr_s, band_k_s, band_v_s, ret_s, conv_s = [jnp.stack(z) for z in zip(*new_s)]
    return (y_prompt, y_sample, mla_ckv_p, mla_kr_p, band_k_p, band_v_p, ret_p, conv_p,
            mla_ckv_s, mla_kr_s, band_k_s, band_v_s, ret_s, conv_s)
```

```python
import functools

import numpy as np
import jax
import jax.numpy as jnp
from jax import lax
from jax.experimental import pallas as pl
from jax.experimental.pallas import tpu as pltpu

CHUNK = 64
HA, DN, DR, DVA = 8, 128, 64, 128
Q_LORA, KV_LORA = 512, 256
HB, DHB = 8, 128
BAND_PREV, REL_CLIP = 8, 128
HR, DKR, DVR = 8, 64, 128
CONV_W = 3
ROPE_THETA = 10000.0
EPS = 1e-5
LANE = 128
QK_CAT = DN + 2 * DR
NEG = -1e30
VMEM_LIMIT = 56 * 1024 * 1024

_BF = jnp.bfloat16
_F32 = jnp.float32


def _dot(a, b):
    return jnp.dot(a, b, preferred_element_type=_F32)


def _dot_nt(a, b):
    return lax.dot_general(a, b, (((1,), (1,)), ((), ())), preferred_element_type=_F32)


def _dot_tn(a, b):
    return lax.dot_general(a, b, (((0,), (0,)), ((), ())), preferred_element_type=_F32)


def _tile(n, pref, step=8):
    if n <= pref:
        return n
    t = pref - pref % step
    while n % t:
        t -= step
    return t


def _params(*sem):
    return pltpu.CompilerParams(dimension_semantics=sem, vmem_limit_bytes=VMEM_LIMIT)


def _swap32(x):
    n = x.shape[-1]
    lane = lax.broadcasted_iota(jnp.int32, x.shape, 1)
    fwd = pltpu.roll(x, n - DR // 2, 1)
    bwd = pltpu.roll(x, DR // 2, 1)
    return jnp.where((lane % DR) < DR // 2, fwd, bwd)


def _rope128(x, cos, sin):
    return x * cos + _swap32(x) * sin


def _layer_norm(z, g, b):
    mu = jnp.mean(z, -1, keepdims=True)
    zc = z - mu
    var = jnp.mean(zc * zc, -1, keepdims=True)
    return zc * lax.rsqrt(var + EPS) * g + b


def _rms(x, g):
    return x * lax.rsqrt(jnp.mean(x * x, -1, keepdims=True) + EPS) * g


def _ada_kernel(c_ref, w_ref, b_ref, o_ref):
    c = c_ref[...]
    s = (c * jax.nn.sigmoid(c)).astype(_BF)
    o_ref[...] = _dot(s, w_ref[...]) + b_ref[...]


def _mod_kernel(x_ref, sc_ref, sh_ref, o_ref):
    o_ref[...] = (x_ref[...] * sc_ref[...] + sh_ref[...]).astype(o_ref.dtype)


def _mm_kernel(x_ref, w_ref, o_ref):
    o_ref[...] = _dot(x_ref[...], w_ref[...]).astype(o_ref.dtype)


def _retqk_kernel(x_ref, w_ref, cos_ref, sin_ref, o_ref):
    y = _dot(x_ref[...], w_ref[...])
    cos, sin = cos_ref[...], sin_ref[...]
    nq = HR * DKR // LANE
    for c in range(2 * nq):
        blk = _rope128(y[:, c * LANE:(c + 1) * LANE], cos, sin)
        if c < nq:
            blk = blk * (DKR ** -0.5)
        o_ref[:, c * LANE:(c + 1) * LANE] = blk


def _lat_kernel(x_ref, wl_ref, gq_ref, gkv_ref, wuq_ref, cos_ref, sin_ref,
                q_ref, ckv_ref, kr_ref):
    y = _dot(x_ref[...], wl_ref[...])
    cos, sin = cos_ref[...], sin_ref[...]
    cqn = _rms(y[:, :Q_LORA], gq_ref[...])
    ckv_ref[...] = _rms(y[:, Q_LORA:Q_LORA + KV_LORA], gkv_ref[...])
    kr_ref[...] = _rope128(y[:, Q_LORA + KV_LORA:], cos, sin)
    qa = _dot(cqn.astype(_BF), wuq_ref[...])
    for h in range(HA):
        o = h * QK_CAT
        q_ref[:, o:o + DN] = qa[:, o:o + DN].astype(_BF)
        q_ref[:, o + DN:o + QK_CAT] = _rope128(qa[:, o + DN:o + QK_CAT], cos, sin).astype(_BF)


def _kvup_kernel(ckv_ref, kr_ref, wk_ref, wv_ref, k_ref, v_ref):
    c = ckv_ref[...].astype(_BF)
    kn = _dot(c, wk_ref[...])
    v_ref[...] = _dot(c, wv_ref[...]).astype(_BF)
    krb = kr_ref[...].astype(_BF)
    for h in range(HA):
        o = h * QK_CAT
        k_ref[:, o:o + DN] = kn[:, h * DN:(h + 1) * DN].astype(_BF)
        k_ref[:, o + DN:o + QK_CAT] = krb


def _mla_kernel(q_ref, k_ref, v_ref, o_ref, *, hg, tq, scale):
    qi = pl.program_id(2)
    row = lax.broadcasted_iota(jnp.int32, (tq, tq), 0) // CHUNK
    col = lax.broadcasted_iota(jnp.int32, (tq, tq), 1) // CHUNK
    diag_ok = col <= row
    for h in range(hg):
        q = q_ref[:, h * QK_CAT:(h + 1) * QK_CAT]

        def step(kt, carry, masked=False, h=h, q=q):
            m, l, acc = carry
            start = pl.multiple_of(kt * tq, tq)
            k = k_ref[pl.ds(start, tq), h * QK_CAT:(h + 1) * QK_CAT]
            v = v_ref[pl.ds(start, tq), h * DVA:(h + 1) * DVA]
            s = _dot_nt(q, k) * scale
            if masked:
                s = jnp.where(diag_ok, s, NEG)
            m_new = jnp.maximum(m, jnp.max(s, -1, keepdims=True))
            p = jnp.exp(s - m_new)
            a = jnp.exp(m - m_new)
            l = a * l + jnp.sum(p, -1, keepdims=True)
            acc = a * acc + _dot(p.astype(_BF), v)
            return m_new, l, acc

        init = (jnp.full((tq, 1), NEG, _F32), jnp.zeros((tq, 1), _F32),
                jnp.zeros((tq, DVA), _F32))
        carry = lax.fori_loop(0, qi, step, init)
        _, l, acc = step(qi, carry, masked=True)
        o_ref[:, h * DVA:(h + 1) * DVA] = (acc / l).astype(o_ref.dtype)


def _band_kernel(q_ref, k_ref, v_ref, tb_ref, o_ref, *, hg, win, chunks_per_tile, scale):
    n = pl.program_id(2)
    ws = pl.multiple_of(jnp.maximum(n * chunks_per_tile - BAND_PREV, 0) * CHUNK, CHUNK)
    for h in range(hg):
        q = q_ref[:, h * DHB:(h + 1) * DHB]
        k = k_ref[pl.ds(ws, win), h * DHB:(h + 1) * DHB]
        v = v_ref[pl.ds(ws, win), h * DHB:(h + 1) * DHB]
        s = _dot_nt(q, k) * scale + tb_ref[h]
        m = jnp.max(s, -1, keepdims=True)
        p = jnp.exp(s - m)
        l = jnp.sum(p, -1, keepdims=True)
        o_ref[:, h * DHB:(h + 1) * DHB] = (_dot(p.astype(_BF), v) / l).astype(o_ref.dtype)


def _attn2_kernel(q_ref, k1_ref, v1_ref, k2_ref, v2_ref, t1_ref, t2_ref, o_ref,
                  *, heads, dq, dv, scale, per_head_table):
    for h in range(heads):
        th = h if per_head_table else 0
        q = q_ref[:, h * dq:(h + 1) * dq]
        k1 = k1_ref[:, h * dq:(h + 1) * dq].astype(_BF)
        k2 = k2_ref[:, h * dq:(h + 1) * dq].astype(_BF)
        v1 = v1_ref[:, h * dv:(h + 1) * dv].astype(_BF)
        v2 = v2_ref[:, h * dv:(h + 1) * dv].astype(_BF)
        s1 = _dot_nt(q, k1) * scale + t1_ref[th]
        s2 = _dot_nt(q, k2) * scale + t2_ref[th]
        m = jnp.maximum(jnp.max(s1, -1, keepdims=True), jnp.max(s2, -1, keepdims=True))
        p1 = jnp.exp(s1 - m)
        p2 = jnp.exp(s2 - m)
        l = jnp.sum(p1, -1, keepdims=True) + jnp.sum(p2, -1, keepdims=True)
        o = _dot(p1.astype(_BF), v1) + _dot(p2.astype(_BF), v2)
        o_ref[:, h * dv:(h + 1) * dv] = (o / l).astype(o_ref.dtype)


def _ret_kernel(qk_ref, v_ref, gr_ref, s0_ref, dq_ref, dk_ref, dm_ref, ds_ref, grn_ref,
                o_ref, sout_ref, s_scr):
    ci = pl.program_id(1)
    npair = HR // 2
    r2 = lax.broadcasted_iota(jnp.int32, (2 * DKR, 2 * DVR), 0) // DKR
    c2 = lax.broadcasted_iota(jnp.int32, (2 * DKR, 2 * DVR), 1) // DVR
    on_diag = r2 == c2

    @pl.when(ci == 0)
    def _():
        for p in range(npair):
            top = jnp.concatenate([s0_ref[2 * p], jnp.zeros((DKR, DVR), _F32)], axis=1)
            bot = jnp.concatenate([jnp.zeros((DKR, DVR), _F32), s0_ref[2 * p + 1]], axis=1)
            s_scr[p] = jnp.concatenate([top, bot], axis=0)

    lane = lax.broadcasted_iota(jnp.int32, (qk_ref.shape[0], LANE), 1)
    for p in range(npair):
        q2 = qk_ref[:, p * LANE:(p + 1) * LANE]
        k2 = qk_ref[:, HR * DKR + p * LANE:HR * DKR + (p + 1) * LANE]
        v2 = v_ref[:, 2 * p * DVR:(2 * p + 2) * DVR]
        k2b = k2.astype(_BF)
        s2 = s_scr[p]
        cross = _dot((q2 * dq_ref[p]).astype(_BF), s2.astype(_BF))
        for e in range(2):
            h = 2 * p + e
            qm = jnp.where((lane // DKR) == e, q2, 0.0).astype(_BF)
            inner = _dot_nt(qm, k2b) * dm_ref[h]
            o = _dot(inner.astype(_BF), v2[:, e * DVR:(e + 1) * DVR])
            o = o + cross[:, e * DVR:(e + 1) * DVR]
            mu = jnp.mean(o, -1, keepdims=True)
            oc = o - mu
            var = jnp.mean(oc * oc, -1, keepdims=True)
            on = oc * lax.rsqrt(var + EPS) * grn_ref[:, h * DVR:(h + 1) * DVR]
            g = gr_ref[:, h * DVR:(h + 1) * DVR]
            o_ref[:, h * DVR:(h + 1) * DVR] = (g * jax.nn.sigmoid(g) * on).astype(o_ref.dtype)
        upd = _dot_tn((k2 * dk_ref[p]).astype(_BF), v2)
        s_scr[p] = s2 * ds_ref[p] + jnp.where(on_diag, upd, 0.0)

    @pl.when(ci == pl.num_programs(1) - 1)
    def _():
        for p in range(npair):
            sp = s_scr[p]
            sout_ref[2 * p] = sp[:DKR, :DVR]
            sout_ref[2 * p + 1] = sp[DKR:, DVR:]


def _merge_kernel(h_ref, oa_ref, ob_ref, oc_ref, wga_ref, wgb_ref, wgc_ref,
                  wpa_ref, wpb_ref, wpc_ref, o_ref):
    h = h_ref[...]

    def branch(o_r, wg_r, wp_r):
        return jax.nn.sigmoid(_dot(h, wg_r[...])) * _dot(o_r[...], wp_r[...])

    m = branch(oa_ref, wga_ref, wpa_ref) + branch(ob_ref, wgb_ref, wpb_ref)
    m = m + branch(oc_ref, wgc_ref, wpc_ref)
    o_ref[...] = m.astype(o_ref.dtype)


def _out_kernel(mg_ref, wo_ref, x_ref, gt_ref, lng_ref, lnb_ref, sc_ref, sh_ref,
                x1_ref, h2_ref, *, alpha):
    y = _dot(mg_ref[...], wo_ref[...])
    x1 = _layer_norm(alpha * x_ref[...] + gt_ref[...] * y, lng_ref[...], lnb_ref[...])
    x1_ref[...] = x1
    h2_ref[...] = (x1 * sc_ref[...] + sh_ref[...]).astype(h2_ref.dtype)


def _ffn_tail(a, a1, a2, bb, cw_ref, cb_ref, wd_ref, x_ref, gt_ref, lng_ref, lnb_ref,
              sc_ref, sh_ref, y_ref, hn_ref, acc_ref, alpha):
    j = pl.program_id(1)
    conv = cb_ref[...] + a2 * cw_ref[0:1, :] + a1 * cw_ref[1:2, :] + a * cw_ref[2:3, :]
    gelu = 0.5 * conv * (1.0 + lax.erf(conv * (2.0 ** -0.5)))
    part = _dot((gelu * bb).astype(_BF), wd_ref[...])

    @pl.when(j == 0)
    def _():
        acc_ref[...] = part

    @pl.when(j > 0)
    def _():
        acc_ref[...] += part

    @pl.when(j == pl.num_programs(1) - 1)
    def _():
        z = alpha * x_ref[...] + gt_ref[...] * acc_ref[...]
        y = _layer_norm(z, lng_ref[...], lnb_ref[...])
        y_ref[...] = y
        hn_ref[...] = (y * sc_ref[...] + sh_ref[...]).astype(hn_ref.dtype)


def _ffn_halo_kernel(h_ref, halo_ref, x_ref, wa_ref, wb_ref, cw_ref, cb_ref, wd_ref, gt_ref,
                     lng_ref, lnb_ref, sc_ref, sh_ref, y_ref, hn_ref, at_ref, acc_ref,
                     *, tiles_per_seq, alpha):
    i = pl.program_id(0)
    h = h_ref[...]
    a = _dot(h, wa_ref[...])
    bb = _dot(h, wb_ref[...])
    ah = _dot(halo_ref[...], wa_ref[...])
    keep = jnp.where(i % tiles_per_seq == 0, 0.0, 1.0)
    nh = ah.shape[0]
    p1 = ah[nh - 1:nh, :] * keep
    p2 = ah[nh - 2:nh - 1, :] * keep
    row = lax.broadcasted_iota(jnp.int32, a.shape, 0)
    a1 = jnp.where(row == 0, p1, pltpu.roll(a, 1, 0))
    a2 = jnp.where(row == 0, p2, jnp.where(row == 1, p1, pltpu.roll(a, 2, 0)))
    at_ref[...] = a[a.shape[0] - at_ref.shape[0]:, :]
    _ffn_tail(a, a1, a2, bb, cw_ref, cb_ref, wd_ref, x_ref, gt_ref, lng_ref, lnb_ref,
              sc_ref, sh_ref, y_ref, hn_ref, acc_ref, alpha)


def _ffn_seg_kernel(h_ref, p1_ref, p2_ref, x_ref, wa_ref, wb_ref, cw_ref, cb_ref, wd_ref,
                    gt_ref, lng_ref, lnb_ref, sc_ref, sh_ref, y_ref, hn_ref, at_ref, acc_ref,
                    *, seg, alpha):
    h = h_ref[...]
    a = _dot(h, wa_ref[...])
    bb = _dot(h, wb_ref[...])
    rs = lax.broadcasted_iota(jnp.int32, a.shape, 0) % seg
    a1 = jnp.where(rs == 0, p1_ref[...], pltpu.roll(a, 1, 0))
    a2 = jnp.where(rs < 2, p2_ref[...], pltpu.roll(a, 2, 0))
    at_ref[...] = a
    _ffn_tail(a, a1, a2, bb, cw_ref, cb_ref, wd_ref, x_ref, gt_ref, lng_ref, lnb_ref,
              sc_ref, sh_ref, y_ref, hn_ref, acc_ref, alpha)


class _Path:
    def __init__(self, nb, t, pos0, dense):
        self.nb, self.t, self.rows, self.pos0, self.dense = nb, t, nb * t, pos0, dense

    def tm(self, pref):
        return _tile(self.rows if self.dense else self.t, pref)

    def mod(self, v, tm):
        d = v.shape[-1]
        if self.dense:
            return jnp.repeat(v, self.t, axis=0).reshape(self.rows // tm, tm, d)
        return v[:, None, :]

    def mod_spec(self, arr, tm, rank):
        g, r, d = arr.shape
        tpg = (self.rows // tm) // g
        if rank == 1:
            return pl.BlockSpec((None, r, d), lambda i: (i // tpg, 0, 0))
        return pl.BlockSpec((None, r, d), lambda i, j: (i // tpg, 0, 0))

    def rope_tables(self):
        half = DR // 2
        inv = ROPE_THETA ** (-jnp.arange(half, dtype=_F32) / half)
        pos = self.pos0 + jnp.arange(self.t, dtype=jnp.int32)
        ang = pos.astype(_F32)[:, None] * inv[None, :]
        cos, sin = jnp.cos(ang), jnp.sin(ang)
        cos = jnp.concatenate([cos, cos, cos, cos], -1)
        sin = jnp.concatenate([-sin, sin, -sin, sin], -1)
        if self.dense:
            cos, sin = jnp.tile(cos, (self.nb, 1)), jnp.tile(sin, (self.nb, 1))
        return cos, sin

    def rope_spec(self, tab, tm, rank):
        nt = tab.shape[0] // tm
        if rank == 1:
            return pl.BlockSpec((tm, LANE), lambda i: (i % nt, 0))
        return pl.BlockSpec((tm, LANE), lambda i, j: (i % nt, 0))


def _row_spec(tm, n):
    return pl.BlockSpec((tm, n), lambda i: (i, 0))


def _full_spec(shape):
    nd = len(shape)
    return pl.BlockSpec(shape, lambda *_: (0,) * nd)


def _mm(x, w, out_dtype, tm, tn, name, row_map=None, out_rows=None):
    rows, k = x.shape
    n = w.shape[1]
    out_rows = rows if out_rows is None else out_rows
    row_map = (lambda i: i) if row_map is None else row_map
    return pl.pallas_call(
        _mm_kernel,
        grid=(out_rows // tm, n // tn),
        in_specs=[pl.BlockSpec((tm, k), lambda i, j: (row_map(i), 0)),
                  pl.BlockSpec((k, tn), lambda i, j: (0, j))],
        out_specs=pl.BlockSpec((tm, tn), lambda i, j: (i, j)),
        out_shape=jax.ShapeDtypeStruct((out_rows, n), out_dtype),
        compiler_params=_params("parallel", "arbitrary"),
        name=name,
    )(x, w)


def _modulate(x, sc, sh, path):
    rows, d = x.shape
    tm = path.tm(1024)
    sc, sh = path.mod(sc, tm), path.mod(sh, tm)
    return pl.pallas_call(
        _mod_kernel,
        grid=(rows // tm,),
        in_specs=[_row_spec(tm, d), path.mod_spec(sc, tm, 1), path.mod_spec(sh, tm, 1)],
        out_specs=_row_spec(tm, d),
        out_shape=jax.ShapeDtypeStruct((rows, d), _BF),
        compiler_params=_params("parallel"),
        name="modulate",
    )(x, sc, sh)


def _latent(h, wl, gq, gkv, wuq, cos, sin, path):
    rows, d = h.shape
    tm = path.tm(512)
    nl = wl.shape[1]
    return pl.pallas_call(
        _lat_kernel,
        grid=(rows // tm,),
        in_specs=[_row_spec(tm, d), _full_spec(wl.shape), _full_spec(gq.shape),
                  _full_spec(gkv.shape), _full_spec(wuq.shape),
                  path.rope_spec(cos, tm, 1), path.rope_spec(sin, tm, 1)],
        out_specs=[_row_spec(tm, HA * QK_CAT), _row_spec(tm, KV_LORA), _row_spec(tm, LANE)],
        out_shape=[jax.ShapeDtypeStruct((rows, HA * QK_CAT), _BF),
                   jax.ShapeDtypeStruct((rows, KV_LORA), _F32),
                   jax.ShapeDtypeStruct((rows, LANE), _F32)],
        compiler_params=_params("parallel"),
        name="mla_latent",
    )(h, wl, gq, gkv, wuq, cos, sin)


def _kvup(ckv, kr, wk, wv):
    rows = ckv.shape[0]
    tm = _tile(rows, 1024)
    return pl.pallas_call(
        _kvup_kernel,
        grid=(rows // tm,),
        in_specs=[_row_spec(tm, KV_LORA), _row_spec(tm, LANE), _full_spec(wk.shape),
                  _full_spec(wv.shape)],
        out_specs=[_row_spec(tm, HA * QK_CAT), _row_spec(tm, HA * DVA)],
        out_shape=[jax.ShapeDtypeStruct((rows, HA * QK_CAT), _BF),
                   jax.ShapeDtypeStruct((rows, HA * DVA), _BF)],
        compiler_params=_params("parallel"),
        name="mla_kv_up",
    )(ckv, kr, wk, wv)


def _mla_prompt(q, k, v, nb, t):
    hg = 2
    tq = _tile(t, 512, CHUNK)
    nq = t // tq
    kern = functools.partial(_mla_kernel, hg=hg, tq=tq, scale=(DN + DR) ** -0.5)
    return pl.pallas_call(
        kern,
        grid=(nb, HA // hg, nq),
        in_specs=[pl.BlockSpec((tq, hg * QK_CAT), lambda b, g, i: (b * nq + i, g)),
                  pl.BlockSpec((t, hg * QK_CAT), lambda b, g, i: (b, g)),
                  pl.BlockSpec((t, hg * DVA), lambda b, g, i: (b, g))],
        out_specs=pl.BlockSpec((tq, hg * DVA), lambda b, g, i: (b * nq + i, g)),
        out_shape=jax.ShapeDtypeStruct((nb * t, HA * DVA), _BF),
        compiler_params=_params("parallel", "parallel", "arbitrary"),
        name="mla_attention",
    )(q, k, v)


def _band_tables(rel_bias, tq, t):
    r = tq // CHUNK
    win = (r + BAND_PREV) * CHUNK
    nv = -(-BAND_PREV // r) + 1
    i = np.arange(tq)[:, None]
    m = np.arange(win)[None, :]
    tabs = []
    for var in range(nv):
        n0 = var * r
        ws = max(n0 - BAND_PREV, 0)
        cq = n0 + i // CHUNK
        ck = ws + m // CHUNK
        allowed = (ck <= cq) & (ck >= cq - BAND_PREV)
        d = np.clip((n0 - ws) * CHUNK + i - m, -REL_CLIP, REL_CLIP) + REL_CLIP
        tabs.append(jnp.where(jnp.asarray(allowed)[None], rel_bias[:, d], NEG))
    return jnp.stack(tabs).astype(_F32), win, nv


def _band_prompt(qkv, rel_bias, nb, t):
    hg = 4
    tq = _tile(t, 256, CHUNK)
    nq = t // tq
    tabs, win, nv = _band_tables(rel_bias, tq, t)
    ng = HB // hg
    kern = functools.partial(_band_kernel, hg=hg, win=win, chunks_per_tile=tq // CHUNK,
                             scale=DHB ** -0.5)
    return pl.pallas_call(
        kern,
        grid=(nb, ng, nq),
        in_specs=[pl.BlockSpec((tq, hg * DHB), lambda b, g, i: (b * nq + i, g)),
                  pl.BlockSpec((t, hg * DHB), lambda b, g, i: (b, ng + g)),
                  pl.BlockSpec((t, hg * DHB), lambda b, g, i: (b, 2 * ng + g)),
                  pl.BlockSpec((None, hg, tq, win),
                               lambda b, g, i: (jnp.minimum(i, nv - 1), g, 0, 0))],
        out_specs=pl.BlockSpec((tq, hg * DHB), lambda b, g, i: (b * nq + i, g)),
        out_shape=jax.ShapeDtypeStruct((nb * t, HB * DHB), _BF),
        compiler_params=_params("parallel", "parallel", "arbitrary"),
        name="band_attention",
    )(qkv, qkv, qkv, tabs)


def _attn2(q, k1, v1, k2, v2, t1, t2, dq, dv, scale, qoff, koff, voff, name):
    nb, tq, _ = q.shape
    n1 = k1.shape[1]
    heads = HA
    kern = functools.partial(_attn2_kernel, heads=heads, dq=dq, dv=dv, scale=scale,
                             per_head_table=t1.shape[0] > 1)
    return pl.pallas_call(
        kern,
        grid=(nb,),
        in_specs=[pl.BlockSpec((None, tq, heads * dq), lambda b: (b, 0, qoff)),
                  pl.BlockSpec((None, n1, heads * dq), lambda b: (b, 0, 0)),
                  pl.BlockSpec((None, n1, heads * dv), lambda b: (b, 0, 0)),
                  pl.BlockSpec((None, tq, heads * dq), lambda b: (b, 0, koff)),
                  pl.BlockSpec((None, tq, heads * dv), lambda b: (b, 0, voff)),
                  _full_spec(t1.shape), _full_spec(t2.shape)],
        out_specs=pl.BlockSpec((None, tq, heads * dv), lambda b: (b, 0, 0)),
        out_shape=jax.ShapeDtypeStruct((nb, tq, heads * dv), _BF),
        compiler_params=_params("parallel"),
        name=name,
    )(q, k1, v1, k2, v2, t1, t2)


def _ret_tables(lc):
    lg = np.log1p(-np.exp2(-5.0 - np.arange(HR, dtype=np.float64)))
    idx = np.arange(lc, dtype=np.float64)
    lane_head = np.arange(LANE) // DKR
    npair = HR // 2
    dq = np.stack([np.exp((idx[:, None] + 1.0) * lg[2 * p + lane_head][None, :])
                   for p in range(npair)])
    dk = np.stack([np.exp((lc - 1.0 - idx)[:, None] * lg[2 * p + lane_head][None, :])
                   for p in range(npair)])
    diff = idx[:, None] - idx[None, :]
    dm = np.stack([np.where(diff >= 0, np.exp(np.maximum(diff, 0.0) * lg[h]), 0.0)
                   for h in range(HR)])
    rh = np.arange(2 * DKR)[:, None] // DKR
    ch = np.arange(2 * DVR)[None, :] // DVR
    ds = np.stack([np.where(rh == ch, np.exp(lc * lg[2 * p + rh]) + 0.0 * ch, 0.0)
                   for p in range(npair)])
    return [jnp.asarray(a, _F32) for a in (dq, dk, dm, ds)]


def _retention(qk, v, gr, s0, grn, nb, t, lc_pref):
    lc = _tile(t, lc_pref, CHUNK) if t > CHUNK else t
    nc = t // lc
    dq, dk, dm, ds = _ret_tables(lc)
    w = HR * DVR
    return pl.pallas_call(
        _ret_kernel,
        grid=(nb, nc),
        in_specs=[pl.BlockSpec((lc, 2 * HR * DKR), lambda b, c: (b * nc + c, 0)),
                  pl.BlockSpec((lc, w), lambda b, c: (b * nc + c, 0)),
                  pl.BlockSpec((lc, w), lambda b, c: (b * nc + c, 0)),
                  pl.BlockSpec((None, HR, DKR, DVR), lambda b, c: (b, 0, 0, 0)),
                  _full_spec(dq.shape), _full_spec(dk.shape), _full_spec(dm.shape),
                  _full_spec(ds.shape), _full_spec(grn.shape)],
        out_specs=[pl.BlockSpec((lc, w), lambda b, c: (b * nc + c, 0)),
                   pl.BlockSpec((None, HR, DKR, DVR), lambda b, c: (b, 0, 0, 0))],
        out_shape=[jax.ShapeDtypeStruct((nb * t, w), _BF),
                   jax.ShapeDtypeStruct((nb, HR, DKR, DVR), _F32)],
        scratch_shapes=[pltpu.VMEM((HR // 2, 2 * DKR, 2 * DVR), _F32)],
        compiler_params=_params("parallel", "arbitrary"),
        name="retention",
    )(qk, v, gr, s0, dq, dk, dm, ds, grn)


def _merge(h, oa, ob, oc, wg, wpa, wpb, wpc, path):
    rows, d = h.shape
    tm = path.tm(512)
    tn = _tile(d, 512, LANE)
    nj = d // tn
    wo = oa.shape[1]
    o_spec = pl.BlockSpec((tm, wo), lambda i, j: (i, 0))
    wp_spec = pl.BlockSpec((wo, tn), lambda i, j: (0, j))
    return pl.pallas_call(
        _merge_kernel,
        grid=(rows // tm, nj),
        in_specs=[pl.BlockSpec((tm, d), lambda i, j: (i, 0)), o_spec, o_spec, o_spec,
                  pl.BlockSpec((d, tn), lambda i, j: (0, j)),
                  pl.BlockSpec((d, tn), lambda i, j: (0, nj + j)),
                  pl.BlockSpec((d, tn), lambda i, j: (0, 2 * nj + j)),
                  wp_spec, wp_spec, wp_spec],
        out_specs=pl.BlockSpec((tm, tn), lambda i, j: (i, j)),
        out_shape=jax.ShapeDtypeStruct((rows, d), _BF),
        compiler_params=_params("parallel", "arbitrary"),
        name="gated_merge",
    )(h, oa, ob, oc, wg, wg, wg, wpa, wpb, wpc)


def _out_proj(mg, wo, x, gt, lng, lnb, sc, sh, path, alpha):
    rows, d = x.shape
    tm = path.tm(512)
    gt, sc, sh = path.mod(gt, tm), path.mod(sc, tm), path.mod(sh, tm)
    return pl.pallas_call(
        functools.partial(_out_kernel, alpha=alpha),
        grid=(rows // tm,),
        in_specs=[_row_spec(tm, d), _full_spec(wo.shape), _row_spec(tm, d),
                  path.mod_spec(gt, tm, 1), _full_spec(lng.shape), _full_spec(lnb.shape),
                  path.mod_spec(sc, tm, 1), path.mod_spec(sh, tm, 1)],
        out_specs=[_row_spec(tm, d), _row_spec(tm, d)],
        out_shape=[jax.ShapeDtypeStruct((rows, d), _F32), jax.ShapeDtypeStruct((rows, d), _BF)],
        compiler_params=_params("parallel"),
        name="out_proj_ln",
    )(mg, wo, x, gt, lng, lnb, sc, sh)


def _ffn(h2, x1, prev, wa, wb, cw, cb, wd, gt, lng, lnb, sc, sh, path, alpha):
    rows, d = x1.shape
    dff = wa.shape[1]
    tm = path.tm(512)
    tf = _tile(dff, 512, LANE)
    nm, nf = rows // tm, dff // tf
    gt, sc, sh = path.mod(gt, tm), path.mod(sc, tm), path.mod(sh, tm)
    row2 = pl.BlockSpec((tm, d), lambda i, j: (i, 0))
    common_in = [row2,
                 pl.BlockSpec((d, tf), lambda i, j: (0, j)),
                 pl.BlockSpec((d, tf), lambda i, j: (0, j)),
                 pl.BlockSpec((CONV_W, tf), lambda i, j: (0, j)),
                 pl.BlockSpec((1, tf), lambda i, j: (0, j)),
                 pl.BlockSpec((tf, d), lambda i, j: (j, 0)),
                 path.mod_spec(gt, tm, 2), pl.BlockSpec((1, d), lambda i, j: (0, 0)),
                 pl.BlockSpec((1, d), lambda i, j: (0, 0)),
                 path.mod_spec(sc, tm, 2), path.mod_spec(sh, tm, 2)]
    common_args = [x1, wa, wb, cw, cb, wd, gt, lng, lnb, sc, sh]
    if path.dense:
        seg = path.t
        rs = jnp.arange(rows) % seg
        sid = jnp.arange(rows) // seg
        p1 = jnp.where((rs == 0)[:, None], prev[sid, 1], 0.0)
        p2 = jnp.where((rs < 2)[:, None], prev[sid, jnp.minimum(rs, 1)], 0.0)
        tail = tm
        kern = functools.partial(_ffn_seg_kernel, seg=seg, alpha=alpha)
        in_specs = [row2, pl.BlockSpec((tm, tf), lambda i, j: (i, j)),
                    pl.BlockSpec((tm, tf), lambda i, j: (i, j))] + common_in
        args = [h2, p1, p2] + common_args
    else:
        hr = 16
        tail = 8
        kern = functools.partial(_ffn_halo_kernel, tiles_per_seq=path.t // tm, alpha=alpha)
        in_specs = [row2, pl.BlockSpec((hr, d), lambda i, j: (jnp.maximum(i * (tm // hr) - 1, 0), 0))
                    ] + common_in
        args = [h2, h2] + common_args
    y, hn, at = pl.pallas_call(
        kern,
        grid=(nm, nf),
        in_specs=in_specs,
        out_specs=[row2, row2, pl.BlockSpec((None, tail, tf), lambda i, j: (i, 0, j))],
        out_shape=[jax.ShapeDtypeStruct((rows, d), _F32), jax.ShapeDtypeStruct((rows, d), _BF),
                   jax.ShapeDtypeStruct((nm, tail, dff), _F32)],
        scratch_shapes=[pltpu.VMEM((tm, d), _F32)],
        compiler_params=_params("parallel", "arbitrary"),
        name="conv_ffn_ln",
    )(*args)
    if path.dense:
        conv_new = at.reshape(path.nb, path.t, dff)[:, path.t - (CONV_W - 1):]
    else:
        tps = path.t // tm
        conv_new = at.reshape(path.nb, tps, tail, dff)[:, tps - 1, tail - (CONV_W - 1):]
    return y, hn, conv_new


def _rel_table(rel_bias, d, allowed):
    idx = np.clip(d, -REL_CLIP, REL_CLIP) + REL_CLIP
    return jnp.where(jnp.asarray(allowed)[None], rel_bias[:, idx], NEG).astype(_F32)


def _layer(x, h, path, mods, nxt, cache, w, alpha):
    rows, d = x.shape
    nb, t = path.nb, path.t
    sh1, sc1, gt1, sh2, sc2, gt2 = mods
    cos, sin = path.rope_tables()
    tm_big = path.tm(1024)

    q_cat, ckv, kr128 = _latent(h, w["w_lat"], w["g_q"], w["g_kv"], w["w_uq"], cos, sin, path)
    k_cat, v_a = _kvup(ckv, kr128, w["w_uk"], w["w_uv"])
    if cache is None:
        oa = _mla_prompt(q_cat, k_cat, v_a, nb, t)
    else:
        past = cache["ckv"].shape[1]
        kc, vc = _kvup(cache["ckv"].reshape(nb * past, KV_LORA),
                       jnp.pad(cache["krope"].reshape(nb * past, DR), ((0, 0), (0, LANE - DR))),
                       w["w_uk"], w["w_uv"])
        q_pos = (path.pos0 + np.arange(t))[:, None] // CHUNK
        ok1 = (np.arange(past)[None, :] // CHUNK) <= q_pos
        ok2 = ((past + np.arange(t))[None, :] // CHUNK) <= q_pos
        t1 = jnp.asarray(np.where(ok1, 0.0, NEG)[None], _F32)
        t2 = jnp.asarray(np.where(ok2, 0.0, NEG)[None], _F32)
        oa = _attn2(q_cat.reshape(nb, t, -1), kc.reshape(nb, past, -1), vc.reshape(nb, past, -1),
                    k_cat.reshape(nb, t, -1), v_a.reshape(nb, t, -1), t1, t2,
                    QK_CAT, DVA, (DN + DR) ** -0.5, 0, 0, 0, "mla_attention_cached")
        oa = oa.reshape(rows, HA * DVA)

    wb3 = HB * DHB
    qkv = _mm(h, w["w_band"], _BF, tm_big, _tile(3 * wb3, 1024, LANE), "band_qkv")
    if cache is None:
        ob = _band_prompt(qkv, w["rel_bias"], nb, t)
        keep = min(BAND_PREV * CHUNK, t)
        tk = _tile(keep, 512)
        per = keep // tk
        kv_new = _mm(h, w["w_band"][:, wb3:], _F32, tk, _tile(2 * wb3, 1024, LANE), "band_kv_tail",
                     row_map=lambda i: (i // per) * (t // tk) + (t - keep) // tk + i % per,
                     out_rows=nb * keep)
        bk_new = kv_new[:, :wb3].reshape(nb, keep, HB, DHB)
        bv_new = kv_new[:, wb3:].reshape(nb, keep, HB, DHB)
    else:
        kv_new = _mm(h, w["w_band"][:, wb3:], _F32, tm_big, _tile(2 * wb3, 1024, LANE), "band_kv_tail")
        bk_new = kv_new[:, :wb3].reshape(nb, t, HB, DHB)
        bv_new = kv_new[:, wb3:].reshape(nb, t, HB, DHB)
        nk = cache["band_k"].shape[1]
        i = np.arange(t)[:, None]
        m1 = np.arange(nk)[None, :]
        m2 = np.arange(t)[None, :]
        t1 = _rel_table(w["rel_bias"], nk + i - m1, (path.pos0 - nk + m1 >= 0) | (i < 0))
        t2 = _rel_table(w["rel_bias"], i - m2, (m2 >= 0) | (i < 0))
        qkv3 = qkv.reshape(nb, t, 3 * wb3)
        ob = _attn2(qkv3, cache["band_k"].reshape(nb, nk, wb3), cache["band_v"].reshape(nb, nk, wb3),
                    qkv3, qkv3, t1, t2, DHB, DHB, DHB ** -0.5, 0, 1, 2, "band_attention_cached")
        ob = ob.reshape(rows, wb3)

    tm_r = path.tm(1024)
    qk_r = pl.pallas_call(
        _retqk_kernel,
        grid=(rows // tm_r,),
        in_specs=[_row_spec(tm_r, d), _full_spec(w["w_retqk"].shape),
                  path.rope_spec(cos, tm_r, 1), path.rope_spec(sin, tm_r, 1)],
        out_specs=_row_spec(tm_r, 2 * HR * DKR),
        out_shape=jax.ShapeDtypeStruct((rows, 2 * HR * DKR), _F32),
        compiler_params=_params("parallel"),
        name="ret_qk_rope",
    )(h, w["w_retqk"], cos, sin)
    v_r = _mm(h, w["w_retv"], _BF, tm_big, HR * DVR, "ret_v")
    g_r = _mm(h, w["w_retg"], _F32, tm_big, HR * DVR, "ret_gate")
    s0 = jnp.zeros((nb, HR, DKR, DVR), _F32) if cache is None else cache["ret"]
    oc, s_new = _retention(qk_r, v_r, g_r, s0, w["g_rn"], nb, t, 256)

    mg = _merge(h, oa, ob, oc, w["w_gates"], w["w_pa"], w["w_pb"], w["w_pc"], path)
    x1, h2 = _out_proj(mg, w["w_o"], x, gt1, w["ln1_g"], w["ln1_b"], sc2, sh2, path, alpha)

    prev = None if cache is None else cache["conv"]
    x2, hn, conv_new = _ffn(h2, x1, prev, w["w_fa"], w["w_fb"], w["cw"], w["cb"], w["w_fd"],
                            gt2, w["ln2_g"], w["ln2_b"], nxt[0], nxt[1], path, alpha)
    state = (ckv.reshape(nb, t, KV_LORA), kr128[:, :DR].reshape(nb, t, DR), bk_new, bv_new,
             s_new, conv_new)
    return x2, hn, state


def _layer_weights(l, wi, w_uq, w_ukv, w_ada_unused, p):
    d = wi.shape[1]
    o = np.cumsum([0, Q_LORA, KV_LORA, DR, HB * DHB, HB * DHB, HB * DHB, HR * DKR, HR * DKR,
                   HR * DVR, HR * DVR, d, d, d]).tolist()
    wl = wi[l]
    lat = jnp.concatenate([wl[:, :o[3]], jnp.zeros((d, LANE - DR), _BF)], axis=1)
    uq = w_uq[l].reshape(Q_LORA, HA, DN + DR)
    uq = jnp.pad(uq, ((0, 0), (0, 0), (0, QK_CAT - DN - DR))).reshape(Q_LORA, HA * QK_CAT)
    ukv = w_ukv[l].reshape(KV_LORA, HA, DN + DVA)
    row = lambda a: a[l][None, :]
    return {
        "w_lat": lat, "w_uq": uq,
        "w_uk": ukv[:, :, :DN].reshape(KV_LORA, HA * DN),
        "w_uv": ukv[:, :, DN:].reshape(KV_LORA, HA * DVA),
        "w_band": wl[:, o[3]:o[6]], "w_retqk": wl[:, o[6]:o[8]], "w_retv": wl[:, o[8]:o[9]],
        "w_retg": wl[:, o[9]:o[10]], "w_gates": wl[:, o[10]:o[13]],
        "g_q": row(p["g_q_lora"]), "g_kv": row(p["g_kv_lora"]), "rel_bias": p["rel_bias"][l],
        "g_rn": row(p["g_ret_norm"]),
        "w_pa": p["w_branch_a"][l], "w_pb": p["w_branch_b"][l], "w_pc": p["w_branch_c"][l],
        "w_o": p["w_o"][l], "ln1_g": row(p["ln1_g"]), "ln1_b": row(p["ln1_b"]),
        "w_fa": p["w_ff_a"][l], "w_fb": p["w_ff_b"][l], "cw": p["conv_w"][l],
        "cb": row(p["conv_b"]), "w_fd": p["w_ff_down"][l],
        "ln2_g": row(p["ln2_g"]), "ln2_b": row(p["ln2_b"]),
    }


def kernel(x_prompt, x_sample, c_prompt, c_sample, cache_mla_ckv, cache_mla_krope, cache_band_k, cache_band_v, state_ret, state_conv, w_ada, b_ada, w_in, g_q_lora, g_kv_lora, w_uq, w_ukv, rel_bias, g_ret_norm, w_branch_a, w_branch_b, w_branch_c, w_o, ln1_g, ln1_b, w_ff_a, w_ff_b, conv_w, conv_b, w_ff_down, ln2_g, ln2_b):
    nb_p, t_p, d = x_prompt.shape
    nb_s, t_s, _ = x_sample.shape
    depth = w_in.shape[0]
    past = cache_mla_ckv.shape[2]
    alpha = (2 * depth) ** 0.25

    nc = nb_p + nb_s
    ncp = -(-nc // 16) * 16
    c_all = jnp.pad(jnp.concatenate([c_prompt, c_sample], 0), ((0, ncp - nc), (0, 0)))
    tn_a = _tile(6 * d, 2048, LANE)
    ada = pl.pallas_call(
        _ada_kernel,
        grid=(depth, 6 * d // tn_a),
        in_specs=[pl.BlockSpec((ncp, d), lambda l, j: (0, 0)),
                  pl.BlockSpec((None, d, tn_a), lambda l, j: (l, 0, j)),
                  pl.BlockSpec((None, 1, tn_a), lambda l, j: (l, 0, j))],
        out_specs=pl.BlockSpec((None, ncp, tn_a), lambda l, j: (l, 0, j)),
        out_shape=jax.ShapeDtypeStruct((depth, ncp, 6 * d), _F32),
        compiler_params=_params("parallel", "arbitrary"),
        name="ada_ln",
    )(c_all, w_ada.astype(_BF), b_ada[:, None, :])

    def mods(l, lo, n):
        a = ada[l, lo:lo + n]
        sh1, sc1, gt1, sh2, sc2, gt2 = [a[:, k * d:(k + 1) * d] for k in range(6)]
        return (sh1, 1.0 + sc1, 1.0 + gt1, sh2, 1.0 + sc2, 1.0 + gt2)

    bf = lambda a: a.astype(_BF)
    p = dict(g_q_lora=g_q_lora, g_kv_lora=g_kv_lora, rel_bias=rel_bias, g_ret_norm=g_ret_norm,
             w_branch_a=bf(w_branch_a), w_branch_b=bf(w_branch_b), w_branch_c=bf(w_branch_c),
             w_o=bf(w_o), ln1_g=ln1_g, ln1_b=ln1_b, w_ff_a=bf(w_ff_a), w_ff_b=bf(w_ff_b),
             conv_w=conv_w, conv_b=conv_b, w_ff_down=bf(w_ff_down), ln2_g=ln2_g, ln2_b=ln2_b)
    wi, wq, wkv = bf(w_in), bf(w_uq), bf(w_ukv)

    streams = [(_Path(nb_p, t_p, 0, False), x_prompt, 0),
               (_Path(nb_s, t_s, past, True), x_sample, nb_p)]
    outs = []
    for path, x0, lo in streams:
        x = x0.reshape(path.rows, d)
        m0 = mods(0, lo, path.nb)
        h = _modulate(x, m0[1], m0[0], path)
        states = []
        for l in range(depth):
            m = mods(l, lo, path.nb)
            mn = mods(min(l + 1, depth - 1), lo, path.nb)
            cache = None
            if path.dense:
                cache = dict(ckv=cache_mla_ckv[l], krope=cache_mla_krope[l], band_k=cache_band_k[l],
                             band_v=cache_band_v[l], ret=state_ret[l], conv=state_conv[l])
            w = _layer_weights(l, wi, wq, wkv, None, p)
            x, h, st = _layer(x, h, path, m, (mn[1], mn[0]), cache, w, alpha)
            states.append(st)
        outs.append((x.reshape(path.nb, path.t, d), [jnp.stack(z) for z in zip(*states)]))
    (y_p, st_p), (y_s, st_s) = outs
    return (y_p, y_s, *st_p, *st_s)
```

```python
import functools

import numpy as np
import jax
import jax.numpy as jnp
from jax import lax
from jax.experimental import pallas as pl
from jax.experimental.pallas import tpu as pltpu

CHUNK = 64
HA, DN, DR, DVA = 8, 128, 64, 128
Q_LORA, KV_LORA = 512, 256
HB, DHB = 8, 128
BAND_PREV, REL_CLIP = 8, 128
HR, DKR, DVR = 8, 64, 128
CONV_W = 3
ROPE_THETA = 10000.0
EPS = 1e-5
LANE = 128
QK_CAT = DN + 2 * DR
NEG = -1e30
VMEM_LIMIT = 56 * 1024 * 1024
MLA_QSCALE = (DN + DR) ** -0.5 * float(np.log2(np.e))
V_CAT = 2 * DVA
FFN_HALO = 8
FFN_ROWS = 64

_BF = jnp.bfloat16
_F32 = jnp.float32


def _dot(a, b):
    return jnp.dot(a, b, preferred_element_type=_F32)


def _dot_nt(a, b):
    return lax.dot_general(a, b, (((1,), (1,)), ((), ())), preferred_element_type=_F32)


def _dot_tn(a, b):
    return lax.dot_general(a, b, (((0,), (0,)), ((), ())), preferred_element_type=_F32)


def _tile(n, pref, step=8):
    if n <= pref:
        return n
    t = pref - pref % step
    while n % t:
        t -= step
    return t


def _params(*sem):
    return pltpu.CompilerParams(dimension_semantics=sem, vmem_limit_bytes=VMEM_LIMIT)


def _swap32(x):
    n = x.shape[-1]
    lane = lax.broadcasted_iota(jnp.int32, x.shape, 1)
    fwd = pltpu.roll(x, n - DR // 2, 1)
    bwd = pltpu.roll(x, DR // 2, 1)
    return jnp.where((lane % DR) < DR // 2, fwd, bwd)


def _rope128(x, cos, sin):
    return x * cos + _swap32(x) * sin


def _layer_norm(z, g, b):
    mu = jnp.mean(z, -1, keepdims=True)
    zc = z - mu
    var = jnp.mean(zc * zc, -1, keepdims=True)
    return zc * lax.rsqrt(var + EPS) * g + b


def _rms(x, g):
    return x * lax.rsqrt(jnp.mean(x * x, -1, keepdims=True) + EPS) * g


def _ada_kernel(c_ref, w_ref, b_ref, o_ref):
    c = c_ref[...]
    s = (c * jax.nn.sigmoid(c)).astype(_BF)
    o_ref[...] = _dot(s, w_ref[...]) + b_ref[...]


def _mod_kernel(x_ref, sc_ref, sh_ref, o_ref):
    o_ref[...] = (x_ref[...] * sc_ref[...] + sh_ref[...]).astype(o_ref.dtype)


def _mm_kernel(x_ref, w_ref, o_ref):
    o_ref[...] = _dot(x_ref[...], w_ref[...]).astype(o_ref.dtype)


def _retqk_kernel(x_ref, w_ref, cos_ref, sin_ref, o_ref):
    y = _dot(x_ref[...], w_ref[...])
    cos, sin = cos_ref[...], sin_ref[...]
    nq = HR * DKR // LANE
    for c in range(2 * nq):
        blk = _rope128(y[:, c * LANE:(c + 1) * LANE], cos, sin)
        if c < nq:
            blk = blk * (DKR ** -0.5)
        o_ref[:, c * LANE:(c + 1) * LANE] = blk


def _lat_kernel(x_ref, wl_ref, gq_ref, gkv_ref, wuq_ref, cos_ref, sin_ref,
                q_ref, ckv_ref, kr_ref):
    y = _dot(x_ref[...], wl_ref[...])
    cos, sin = cos_ref[...], sin_ref[...]
    cqn = _rms(y[:, :Q_LORA], gq_ref[...])
    ckv_ref[...] = _rms(y[:, Q_LORA:Q_LORA + KV_LORA], gkv_ref[...])
    kr_ref[...] = _rope128(y[:, Q_LORA + KV_LORA:], cos, sin)
    qa = _dot(cqn.astype(_BF), wuq_ref[...]) * MLA_QSCALE
    for h in range(HA):
        o = h * QK_CAT
        q_ref[:, o:o + DN] = qa[:, o:o + DN].astype(_BF)
        q_ref[:, o + DN:o + QK_CAT] = _rope128(qa[:, o + DN:o + QK_CAT], cos, sin).astype(_BF)


def _kvup_kernel(ckv_ref, kr_ref, wk_ref, wv_ref, k_ref, v_ref):
    c = ckv_ref[...].astype(_BF)
    kn = _dot(c, wk_ref[...])
    vn = _dot(c, wv_ref[...])
    krb = kr_ref[...].astype(_BF)
    ones = jnp.ones((c.shape[0], DVA), _BF)
    for h in range(HA):
        o = h * QK_CAT
        k_ref[:, o:o + DN] = kn[:, h * DN:(h + 1) * DN].astype(_BF)
        k_ref[:, o + DN:o + QK_CAT] = krb
        v_ref[:, h * V_CAT:h * V_CAT + DVA] = vn[:, h * DVA:(h + 1) * DVA].astype(_BF)
        v_ref[:, h * V_CAT + DVA:(h + 1) * V_CAT] = ones


def _mla_kernel(q_ref, k_ref, v_ref, o_ref, *, hg, tq):
    qi = pl.program_id(2)
    row = lax.broadcasted_iota(jnp.int32, (tq, tq), 0) // CHUNK
    col = lax.broadcasted_iota(jnp.int32, (tq, tq), 1) // CHUNK
    diag_ok = col <= row

    def step(kt, carry, masked=False):
        start = pl.multiple_of(kt * tq, tq)
        out = []
        for h in range(hg):
            m, acc = carry[h]
            q = q_ref[:, h * QK_CAT:(h + 1) * QK_CAT]
            k = k_ref[pl.ds(start, tq), h * QK_CAT:(h + 1) * QK_CAT]
            v = v_ref[pl.ds(start, tq), h * V_CAT:(h + 1) * V_CAT]
            s = _dot_nt(q, k)
            if masked:
                s = jnp.where(diag_ok, s, NEG)
            m_new = jnp.maximum(m, jnp.max(s, -1, keepdims=True))
            p = jnp.exp2(s - m_new)
            acc = jnp.exp2(m - m_new) * acc + _dot(p.astype(_BF), v)
            out.append((m_new, acc))
        return tuple(out)

    init = tuple((jnp.full((tq, 1), NEG, _F32), jnp.zeros((tq, V_CAT), _F32)) for _ in range(hg))
    carry = step(qi, lax.fori_loop(0, qi, step, init), masked=True)
    for h in range(hg):
        acc = carry[h][1]
        o_ref[:, h * DVA:(h + 1) * DVA] = (acc[:, :DVA] / acc[:, DVA:]).astype(o_ref.dtype)


def _band_kernel(q_ref, k_ref, v_ref, tb_ref, o_ref, *, hg, win, chunks_per_tile, scale):
    n = pl.program_id(2)
    ws = pl.multiple_of(jnp.maximum(n * chunks_per_tile - BAND_PREV, 0) * CHUNK, CHUNK)
    for h in range(hg):
        q = q_ref[:, h * DHB:(h + 1) * DHB]
        k = k_ref[pl.ds(ws, win), h * DHB:(h + 1) * DHB]
        v = v_ref[pl.ds(ws, win), h * DHB:(h + 1) * DHB]
        s = _dot_nt(q, k) * scale + tb_ref[h]
        m = jnp.max(s, -1, keepdims=True)
        p = jnp.exp(s - m)
        l = jnp.sum(p, -1, keepdims=True)
        o_ref[:, h * DHB:(h + 1) * DHB] = (_dot(p.astype(_BF), v) / l).astype(o_ref.dtype)


def _attn2_kernel(q_ref, k1_ref, v1_ref, k2_ref, v2_ref, t1_ref, t2_ref, o_ref,
                  *, heads, dq, dv, v_stride, scale, base2, per_head_table):
    ex = jnp.exp2 if base2 else jnp.exp
    for h in range(heads):
        th = h if per_head_table else 0
        q = q_ref[:, h * dq:(h + 1) * dq]
        k1 = k1_ref[:, h * dq:(h + 1) * dq].astype(_BF)
        k2 = k2_ref[:, h * dq:(h + 1) * dq].astype(_BF)
        v1 = v1_ref[:, h * v_stride:h * v_stride + dv].astype(_BF)
        v2 = v2_ref[:, h * v_stride:h * v_stride + dv].astype(_BF)
        s1, s2 = _dot_nt(q, k1), _dot_nt(q, k2)
        if scale is not None:
            s1, s2 = s1 * scale, s2 * scale
        s1, s2 = s1 + t1_ref[th], s2 + t2_ref[th]
        m = jnp.maximum(jnp.max(s1, -1, keepdims=True), jnp.max(s2, -1, keepdims=True))
        p1 = ex(s1 - m)
        p2 = ex(s2 - m)
        l = jnp.sum(p1, -1, keepdims=True) + jnp.sum(p2, -1, keepdims=True)
        o = _dot(p1.astype(_BF), v1) + _dot(p2.astype(_BF), v2)
        o_ref[:, h * dv:(h + 1) * dv] = (o / l).astype(o_ref.dtype)


def _ret_kernel(qk_ref, v_ref, gr_ref, s0_ref, dq_ref, dk_ref, dm_ref, ds_ref, grn_ref,
                o_ref, sout_ref, s_scr):
    ci = pl.program_id(1)
    npair = HR // 2
    r2 = lax.broadcasted_iota(jnp.int32, (2 * DKR, 2 * DVR), 0) // DKR
    c2 = lax.broadcasted_iota(jnp.int32, (2 * DKR, 2 * DVR), 1) // DVR
    on_diag = r2 == c2

    @pl.when(ci == 0)
    def _():
        for p in range(npair):
            top = jnp.concatenate([s0_ref[2 * p], jnp.zeros((DKR, DVR), _F32)], axis=1)
            bot = jnp.concatenate([jnp.zeros((DKR, DVR), _F32), s0_ref[2 * p + 1]], axis=1)
            s_scr[p] = jnp.concatenate([top, bot], axis=0)

    lane = lax.broadcasted_iota(jnp.int32, (qk_ref.shape[0], LANE), 1)
    for p in range(npair):
        q2 = qk_ref[:, p * LANE:(p + 1) * LANE]
        k2 = qk_ref[:, HR * DKR + p * LANE:HR * DKR + (p + 1) * LANE]
        v2 = v_ref[:, 2 * p * DVR:(2 * p + 2) * DVR]
        k2b = k2.astype(_BF)
        s2 = s_scr[p]
        cross = _dot((q2 * dq_ref[p]).astype(_BF), s2.astype(_BF))
        for e in range(2):
            h = 2 * p + e
            qm = jnp.where((lane // DKR) == e, q2, 0.0).astype(_BF)
            inner = _dot_nt(qm, k2b) * dm_ref[h]
            o = _dot(inner.astype(_BF), v2[:, e * DVR:(e + 1) * DVR])
            o = o + cross[:, e * DVR:(e + 1) * DVR]
            mu = jnp.mean(o, -1, keepdims=True)
            oc = o - mu
            var = jnp.mean(oc * oc, -1, keepdims=True)
            on = oc * lax.rsqrt(var + EPS) * grn_ref[:, h * DVR:(h + 1) * DVR]
            g = gr_ref[:, h * DVR:(h + 1) * DVR]
            o_ref[:, h * DVR:(h + 1) * DVR] = (g * jax.nn.sigmoid(g) * on).astype(o_ref.dtype)
        upd = _dot_tn((k2 * dk_ref[p]).astype(_BF), v2)
        s_scr[p] = s2 * ds_ref[p] + jnp.where(on_diag, upd, 0.0)

    @pl.when(ci == pl.num_programs(1) - 1)
    def _():
        for p in range(npair):
            sp = s_scr[p]
            sout_ref[2 * p] = sp[:DKR, :DVR]
            sout_ref[2 * p + 1] = sp[DKR:, DVR:]


def _merge_kernel(h_ref, oa_ref, ob_ref, oc_ref, wga_ref, wgb_ref, wgc_ref,
                  wpa_ref, wpb_ref, wpc_ref, o_ref):
    h = h_ref[...]

    def branch(o_r, wg_r, wp_r):
        return jax.nn.sigmoid(_dot(h, wg_r[...])) * _dot(o_r[...], wp_r[...])

    m = branch(oa_ref, wga_ref, wpa_ref) + branch(ob_ref, wgb_ref, wpb_ref)
    m = m + branch(oc_ref, wgc_ref, wpc_ref)
    o_ref[...] = m.astype(o_ref.dtype)


def _out_kernel(mg_ref, wo_ref, x_ref, gt_ref, lng_ref, lnb_ref, sc_ref, sh_ref,
                x1_ref, h2_ref, *, alpha):
    y = _dot(mg_ref[...], wo_ref[...])
    x1 = _layer_norm(alpha * x_ref[...] + gt_ref[...] * y, lng_ref[...], lnb_ref[...])
    x1_ref[...] = x1
    h2_ref[...] = (x1 * sc_ref[...] + sh_ref[...]).astype(h2_ref.dtype)


def _ffn_tail(a, a1, a2, bb, cw_ref, cb_ref, wd_ref, x_ref, gt_ref, lng_ref, lnb_ref,
              sc_ref, sh_ref, y_ref, hn_ref, acc_ref, alpha):
    _ffn_finish(_gated(a, a1, a2, bb, cw_ref, cb_ref), wd_ref, x_ref, gt_ref, lng_ref, lnb_ref,
                sc_ref, sh_ref, y_ref, hn_ref, acc_ref, alpha)


def _gated(a, a1, a2, bb, cw_ref, cb_ref):
    conv = cb_ref[...] + a2 * cw_ref[0:1, :] + a1 * cw_ref[1:2, :] + a * cw_ref[2:3, :]
    gelu = 0.5 * conv * (1.0 + lax.erf(conv * (2.0 ** -0.5)))
    return (gelu * bb).astype(_BF)


def _ffn_finish(g, wd_ref, x_ref, gt_ref, lng_ref, lnb_ref, sc_ref, sh_ref, y_ref, hn_ref,
                acc_ref, alpha):
    j = pl.program_id(1)
    part = _dot(g, wd_ref[...])

    @pl.when(j == 0)
    def _():
        acc_ref[...] = part

    @pl.when(j > 0)
    def _():
        acc_ref[...] += part

    @pl.when(j == pl.num_programs(1) - 1)
    def _():
        z = alpha * x_ref[...] + gt_ref[...] * acc_ref[...]
        y = _layer_norm(z, lng_ref[...], lnb_ref[...])
        y_ref[...] = y
        hn_ref[...] = (y * sc_ref[...] + sh_ref[...]).astype(hn_ref.dtype)


def _ffn_halo_kernel(h_ref, halo_ref, x_ref, wa_ref, wb_ref, cw_ref, cb_ref, wd_ref, gt_ref,
                     lng_ref, lnb_ref, sc_ref, sh_ref, y_ref, hn_ref, at_ref, acc_ref,
                     a_scr, b_scr, g_scr, *, tiles_per_seq, alpha):
    i = pl.program_id(0)
    h = h_ref[...]
    tm = h.shape[0]
    a_scr[FFN_HALO:, :] = _dot(h, wa_ref[...])
    b_scr[...] = _dot(h, wb_ref[...])
    ah = _dot(halo_ref[...], wa_ref[...])
    keep = jnp.where(i % tiles_per_seq == 0, 0.0, 1.0)
    a_scr[:FFN_HALO, :] = ah[ah.shape[0] - FFN_HALO:, :] * keep
    at_ref[...] = a_scr[tm:, :]
    rc = min(FFN_ROWS, tm)
    for c in range(tm // rc):
        r = FFN_HALO + c * rc
        g_scr[r - FFN_HALO:r - FFN_HALO + rc, :] = _gated(
            a_scr[r:r + rc, :], a_scr[r - 1:r - 1 + rc, :], a_scr[r - 2:r - 2 + rc, :],
            b_scr[r - FFN_HALO:r - FFN_HALO + rc, :], cw_ref, cb_ref)
    _ffn_finish(g_scr[...], wd_ref, x_ref, gt_ref, lng_ref, lnb_ref, sc_ref, sh_ref, y_ref,
                hn_ref, acc_ref, alpha)


def _ffn_seg_kernel(h_ref, p1_ref, p2_ref, x_ref, wa_ref, wb_ref, cw_ref, cb_ref, wd_ref,
                    gt_ref, lng_ref, lnb_ref, sc_ref, sh_ref, y_ref, hn_ref, at_ref, acc_ref,
                    *, seg, alpha):
    h = h_ref[...]
    a = _dot(h, wa_ref[...])
    bb = _dot(h, wb_ref[...])
    rs = lax.broadcasted_iota(jnp.int32, a.shape, 0) % seg
    a1 = jnp.where(rs == 0, p1_ref[...], pltpu.roll(a, 1, 0))
    a2 = jnp.where(rs < 2, p2_ref[...], pltpu.roll(a, 2, 0))
    at_ref[...] = a
    _ffn_tail(a, a1, a2, bb, cw_ref, cb_ref, wd_ref, x_ref, gt_ref, lng_ref, lnb_ref,
              sc_ref, sh_ref, y_ref, hn_ref, acc_ref, alpha)


class _Path:
    def __init__(self, nb, t, pos0, dense):
        self.nb, self.t, self.rows, self.pos0, self.dense = nb, t, nb * t, pos0, dense

    def tm(self, pref):
        return _tile(self.rows if self.dense else self.t, pref)

    def mod(self, v, tm):
        d = v.shape[-1]
        if self.dense:
            return jnp.repeat(v, self.t, axis=0).reshape(self.rows // tm, tm, d)
        return v[:, None, :]

    def mod_spec(self, arr, tm, rank):
        g, r, d = arr.shape
        tpg = (self.rows // tm) // g
        if rank == 1:
            return pl.BlockSpec((None, r, d), lambda i: (i // tpg, 0, 0))
        return pl.BlockSpec((None, r, d), lambda i, j: (i // tpg, 0, 0))

    def rope_tables(self):
        half = DR // 2
        inv = ROPE_THETA ** (-jnp.arange(half, dtype=_F32) / half)
        pos = self.pos0 + jnp.arange(self.t, dtype=jnp.int32)
        ang = pos.astype(_F32)[:, None] * inv[None, :]
        cos, sin = jnp.cos(ang), jnp.sin(ang)
        cos = jnp.concatenate([cos, cos, cos, cos], -1)
        sin = jnp.concatenate([-sin, sin, -sin, sin], -1)
        if self.dense:
            cos, sin = jnp.tile(cos, (self.nb, 1)), jnp.tile(sin, (self.nb, 1))
        return cos, sin

    def rope_spec(self, tab, tm, rank):
        nt = tab.shape[0] // tm
        if rank == 1:
            return pl.BlockSpec((tm, LANE), lambda i: (i % nt, 0))
        return pl.BlockSpec((tm, LANE), lambda i, j: (i % nt, 0))


def _row_spec(tm, n):
    return pl.BlockSpec((tm, n), lambda i: (i, 0))


def _full_spec(shape):
    nd = len(shape)
    return pl.BlockSpec(shape, lambda *_: (0,) * nd)


def _mm(x, w, out_dtype, tm, tn, name, row_map=None, out_rows=None):
    rows, k = x.shape
    n = w.shape[1]
    out_rows = rows if out_rows is None else out_rows
    row_map = (lambda i: i) if row_map is None else row_map
    return pl.pallas_call(
        _mm_kernel,
        grid=(out_rows // tm, n // tn),
        in_specs=[pl.BlockSpec((tm, k), lambda i, j: (row_map(i), 0)),
                  pl.BlockSpec((k, tn), lambda i, j: (0, j))],
        out_specs=pl.BlockSpec((tm, tn), lambda i, j: (i, j)),
        out_shape=jax.ShapeDtypeStruct((out_rows, n), out_dtype),
        compiler_params=_params("parallel", "arbitrary"),
        name=name,
    )(x, w)


def _modulate(x, sc, sh, path):
    rows, d = x.shape
    tm = path.tm(1024)
    sc, sh = path.mod(sc, tm), path.mod(sh, tm)
    return pl.pallas_call(
        _mod_kernel,
        grid=(rows // tm,),
        in_specs=[_row_spec(tm, d), path.mod_spec(sc, tm, 1), path.mod_spec(sh, tm, 1)],
        out_specs=_row_spec(tm, d),
        out_shape=jax.ShapeDtypeStruct((rows, d), _BF),
        compiler_params=_params("parallel"),
        name="modulate",
    )(x, sc, sh)


def _latent(h, wl, gq, gkv, wuq, cos, sin, path):
    rows, d = h.shape
    tm = path.tm(512)
    nl = wl.shape[1]
    return pl.pallas_call(
        _lat_kernel,
        grid=(rows // tm,),
        in_specs=[_row_spec(tm, d), _full_spec(wl.shape), _full_spec(gq.shape),
                  _full_spec(gkv.shape), _full_spec(wuq.shape),
                  path.rope_spec(cos, tm, 1), path.rope_spec(sin, tm, 1)],
        out_specs=[_row_spec(tm, HA * QK_CAT), _row_spec(tm, KV_LORA), _row_spec(tm, LANE)],
        out_shape=[jax.ShapeDtypeStruct((rows, HA * QK_CAT), _BF),
                   jax.ShapeDtypeStruct((rows, KV_LORA), _F32),
                   jax.ShapeDtypeStruct((rows, LANE), _F32)],
        compiler_params=_params("parallel"),
        name="mla_latent",
    )(h, wl, gq, gkv, wuq, cos, sin)


def _kvup(ckv, kr, wk, wv):
    rows = ckv.shape[0]
    tm = _tile(rows, 1024)
    return pl.pallas_call(
        _kvup_kernel,
        grid=(rows // tm,),
        in_specs=[_row_spec(tm, KV_LORA), _row_spec(tm, LANE), _full_spec(wk.shape),
                  _full_spec(wv.shape)],
        out_specs=[_row_spec(tm, HA * QK_CAT), _row_spec(tm, HA * V_CAT)],
        out_shape=[jax.ShapeDtypeStruct((rows, HA * QK_CAT), _BF),
                   jax.ShapeDtypeStruct((rows, HA * V_CAT), _BF)],
        compiler_params=_params("parallel"),
        name="mla_kv_up",
    )(ckv, kr, wk, wv)


def _mla_prompt(q, k, v, nb, t):
    hg = 2
    tq = _tile(t, 512, CHUNK)
    nq = t // tq
    kern = functools.partial(_mla_kernel, hg=hg, tq=tq)
    return pl.pallas_call(
        kern,
        grid=(nb, HA // hg, nq),
        in_specs=[pl.BlockSpec((tq, hg * QK_CAT), lambda b, g, i: (b * nq + i, g)),
                  pl.BlockSpec((t, hg * QK_CAT), lambda b, g, i: (b, g)),
                  pl.BlockSpec((t, hg * V_CAT), lambda b, g, i: (b, g))],
        out_specs=pl.BlockSpec((tq, hg * DVA), lambda b, g, i: (b * nq + i, g)),
        out_shape=jax.ShapeDtypeStruct((nb * t, HA * DVA), _BF),
        compiler_params=_params("parallel", "parallel", "arbitrary"),
        name="mla_attention",
    )(q, k, v)


def _band_tables(rel_bias, tq, t):
    r = tq // CHUNK
    win = (r + BAND_PREV) * CHUNK
    nv = -(-BAND_PREV // r) + 1
    i = np.arange(tq)[:, None]
    m = np.arange(win)[None, :]
    tabs = []
    for var in range(nv):
        n0 = var * r
        ws = max(n0 - BAND_PREV, 0)
        cq = n0 + i // CHUNK
        ck = ws + m // CHUNK
        allowed = (ck <= cq) & (ck >= cq - BAND_PREV)
        tabs.append(_rel_table(rel_bias, (n0 - ws) * CHUNK, tq, win, allowed))
    return jnp.stack(tabs), win, nv


def _rel_table(rel_bias, off, nq, nk, allowed):
    period = nq + nk
    u = np.arange(period)
    u = np.where(u < nk, u, u - period)
    f = rel_bias[:, np.clip(off - u, -REL_CLIP, REL_CLIP) + REL_CLIP]
    circ = jnp.tile(f, (1, nq))[:, :nq * (period - 1)].reshape(-1, nq, period - 1)[:, :, :nk]
    return jnp.where(jnp.asarray(np.broadcast_to(allowed, (nq, nk)))[None], circ, NEG).astype(_F32)


def _band_prompt(qkv, rel_bias, nb, t):
    hg = 4
    tq = _tile(t, 256, CHUNK)
    nq = t // tq
    tabs, win, nv = _band_tables(rel_bias, tq, t)
    ng = HB // hg
    kern = functools.partial(_band_kernel, hg=hg, win=win, chunks_per_tile=tq // CHUNK,
                             scale=DHB ** -0.5)
    return pl.pallas_call(
        kern,
        grid=(nb, ng, nq),
        in_specs=[pl.BlockSpec((tq, hg * DHB), lambda b, g, i: (b * nq + i, g)),
                  pl.BlockSpec((t, hg * DHB), lambda b, g, i: (b, ng + g)),
                  pl.BlockSpec((t, hg * DHB), lambda b, g, i: (b, 2 * ng + g)),
                  pl.BlockSpec((None, hg, tq, win),
                               lambda b, g, i: (jnp.minimum(i, nv - 1), g, 0, 0))],
        out_specs=pl.BlockSpec((tq, hg * DHB), lambda b, g, i: (b * nq + i, g)),
        out_shape=jax.ShapeDtypeStruct((nb * t, HB * DHB), _BF),
        compiler_params=_params("parallel", "parallel", "arbitrary"),
        name="band_attention",
    )(qkv, qkv, qkv, tabs)


def _attn2(q, k1, v1, k2, v2, t1, t2, dq, dv, v_stride, scale, base2, qoff, koff, voff, name):
    nb, tq, _ = q.shape
    n1 = k1.shape[1]
    heads = HA
    kern = functools.partial(_attn2_kernel, heads=heads, dq=dq, dv=dv, v_stride=v_stride,
                             scale=scale, base2=base2, per_head_table=t1.shape[0] > 1)
    return pl.pallas_call(
        kern,
        grid=(nb,),
        in_specs=[pl.BlockSpec((None, tq, heads * dq), lambda b: (b, 0, qoff)),
                  pl.BlockSpec((None, n1, heads * dq), lambda b: (b, 0, 0)),
                  pl.BlockSpec((None, n1, heads * v_stride), lambda b: (b, 0, 0)),
                  pl.BlockSpec((None, tq, heads * dq), lambda b: (b, 0, koff)),
                  pl.BlockSpec((None, tq, heads * v_stride), lambda b: (b, 0, voff)),
                  _full_spec(t1.shape), _full_spec(t2.shape)],
        out_specs=pl.BlockSpec((None, tq, heads * dv), lambda b: (b, 0, 0)),
        out_shape=jax.ShapeDtypeStruct((nb, tq, heads * dv), _BF),
        compiler_params=_params("parallel"),
        name=name,
    )(q, k1, v1, k2, v2, t1, t2)


def _ret_tables(lc):
    lg = np.log1p(-np.exp2(-5.0 - np.arange(HR, dtype=np.float64)))
    idx = np.arange(lc, dtype=np.float64)
    lane_head = np.arange(LANE) // DKR
    npair = HR // 2
    dq = np.stack([np.exp((idx[:, None] + 1.0) * lg[2 * p + lane_head][None, :])
                   for p in range(npair)])
    dk = np.stack([np.exp((lc - 1.0 - idx)[:, None] * lg[2 * p + lane_head][None, :])
                   for p in range(npair)])
    diff = idx[:, None] - idx[None, :]
    dm = np.stack([np.where(diff >= 0, np.exp(np.maximum(diff, 0.0) * lg[h]), 0.0)
                   for h in range(HR)])
    rh = np.arange(2 * DKR)[:, None] // DKR
    ch = np.arange(2 * DVR)[None, :] // DVR
    ds = np.stack([np.where(rh == ch, np.exp(lc * lg[2 * p + rh]) + 0.0 * ch, 0.0)
                   for p in range(npair)])
    return [jnp.asarray(a, _F32) for a in (dq, dk, dm, ds)]


def _retention(qk, v, gr, s0, grn, nb, t, lc_pref):
    lc = _tile(t, lc_pref, CHUNK) if t > CHUNK else t
    nc = t // lc
    dq, dk, dm, ds = _ret_tables(lc)
    w = HR * DVR
    return pl.pallas_call(
        _ret_kernel,
        grid=(nb, nc),
        in_specs=[pl.BlockSpec((lc, 2 * HR * DKR), lambda b, c: (b * nc + c, 0)),
                  pl.BlockSpec((lc, w), lambda b, c: (b * nc + c, 0)),
                  pl.BlockSpec((lc, w), lambda b, c: (b * nc + c, 0)),
                  pl.BlockSpec((None, HR, DKR, DVR), lambda b, c: (b, 0, 0, 0)),
                  _full_spec(dq.shape), _full_spec(dk.shape), _full_spec(dm.shape),
                  _full_spec(ds.shape), _full_spec(grn.shape)],
        out_specs=[pl.BlockSpec((lc, w), lambda b, c: (b * nc + c, 0)),
                   pl.BlockSpec((None, HR, DKR, DVR), lambda b, c: (b, 0, 0, 0))],
        out_shape=[jax.ShapeDtypeStruct((nb * t, w), _BF),
                   jax.ShapeDtypeStruct((nb, HR, DKR, DVR), _F32)],
        scratch_shapes=[pltpu.VMEM((HR // 2, 2 * DKR, 2 * DVR), _F32)],
        compiler_params=_params("parallel", "arbitrary"),
        name="retention",
    )(qk, v, gr, s0, dq, dk, dm, ds, grn)


def _merge(h, oa, ob, oc, wg, wpa, wpb, wpc, path):
    rows, d = h.shape
    tm = path.tm(512)
    tn = _tile(d, 512, LANE)
    nj = d // tn
    wo = oa.shape[1]
    o_spec = pl.BlockSpec((tm, wo), lambda i, j: (i, 0))
    wp_spec = pl.BlockSpec((wo, tn), lambda i, j: (0, j))
    return pl.pallas_call(
        _merge_kernel,
        grid=(rows // tm, nj),
        in_specs=[pl.BlockSpec((tm, d), lambda i, j: (i, 0)), o_spec, o_spec, o_spec,
                  pl.BlockSpec((d, tn), lambda i, j: (0, j)),
                  pl.BlockSpec((d, tn), lambda i, j: (0, nj + j)),
                  pl.BlockSpec((d, tn), lambda i, j: (0, 2 * nj + j)),
                  wp_spec, wp_spec, wp_spec],
        out_specs=pl.BlockSpec((tm, tn), lambda i, j: (i, j)),
        out_shape=jax.ShapeDtypeStruct((rows, d), _BF),
        compiler_params=_params("parallel", "arbitrary"),
        name="gated_merge",
    )(h, oa, ob, oc, wg, wg, wg, wpa, wpb, wpc)


def _out_proj(mg, wo, x, gt, lng, lnb, sc, sh, path, alpha):
    rows, d = x.shape
    tm = path.tm(512)
    gt, sc, sh = path.mod(gt, tm), path.mod(sc, tm), path.mod(sh, tm)
    return pl.pallas_call(
        functools.partial(_out_kernel, alpha=alpha),
        grid=(rows // tm,),
        in_specs=[_row_spec(tm, d), _full_spec(wo.shape), _row_spec(tm, d),
                  path.mod_spec(gt, tm, 1), _full_spec(lng.shape), _full_spec(lnb.shape),
                  path.mod_spec(sc, tm, 1), path.mod_spec(sh, tm, 1)],
        out_specs=[_row_spec(tm, d), _row_spec(tm, d)],
        out_shape=[jax.ShapeDtypeStruct((rows, d), _F32), jax.ShapeDtypeStruct((rows, d), _BF)],
        compiler_params=_params("parallel"),
        name="out_proj_ln",
    )(mg, wo, x, gt, lng, lnb, sc, sh)


def _ffn(h2, x1, prev, wa, wb, cw, cb, wd, gt, lng, lnb, sc, sh, path, alpha):
    rows, d = x1.shape
    dff = wa.shape[1]
    tm = path.tm(512)
    tf = _tile(dff, 512, LANE)
    nm, nf = rows // tm, dff // tf
    gt, sc, sh = path.mod(gt, tm), path.mod(sc, tm), path.mod(sh, tm)
    row2 = pl.BlockSpec((tm, d), lambda i, j: (i, 0))
    common_in = [row2,
                 pl.BlockSpec((d, tf), lambda i, j: (0, j)),
                 pl.BlockSpec((d, tf), lambda i, j: (0, j)),
                 pl.BlockSpec((CONV_W, tf), lambda i, j: (0, j)),
                 pl.BlockSpec((1, tf), lambda i, j: (0, j)),
                 pl.BlockSpec((tf, d), lambda i, j: (j, 0)),
                 path.mod_spec(gt, tm, 2), pl.BlockSpec((1, d), lambda i, j: (0, 0)),
                 pl.BlockSpec((1, d), lambda i, j: (0, 0)),
                 path.mod_spec(sc, tm, 2), path.mod_spec(sh, tm, 2)]
    common_args = [x1, wa, wb, cw, cb, wd, gt, lng, lnb, sc, sh]
    if path.dense:
        seg = path.t
        rs = jnp.arange(rows) % seg
        sid = jnp.arange(rows) // seg
        p1 = jnp.where((rs == 0)[:, None], prev[sid, 1], 0.0)
        p2 = jnp.where((rs < 2)[:, None], prev[sid, jnp.minimum(rs, 1)], 0.0)
        tail = tm
        kern = functools.partial(_ffn_seg_kernel, seg=seg, alpha=alpha)
        in_specs = [row2, pl.BlockSpec((tm, tf), lambda i, j: (i, j)),
                    pl.BlockSpec((tm, tf), lambda i, j: (i, j))] + common_in
        args = [h2, p1, p2] + common_args
    else:
        hr = 16
        tail = 8
        kern = functools.partial(_ffn_halo_kernel, tiles_per_seq=path.t // tm, alpha=alpha)
        in_specs = [row2, pl.BlockSpec((hr, d), lambda i, j: (jnp.maximum(i * (tm // hr) - 1, 0), 0))
                    ] + common_in
        args = [h2, h2] + common_args
    y, hn, at = pl.pallas_call(
        kern,
        grid=(nm, nf),
        in_specs=in_specs,
        out_specs=[row2, row2, pl.BlockSpec((None, tail, tf), lambda i, j: (i, 0, j))],
        out_shape=[jax.ShapeDtypeStruct((rows, d), _F32), jax.ShapeDtypeStruct((rows, d), _BF),
                   jax.ShapeDtypeStruct((nm, tail, dff), _F32)],
        scratch_shapes=[pltpu.VMEM((tm, d), _F32)] + ([] if path.dense else [
            pltpu.VMEM((tm + FFN_HALO, tf), _F32), pltpu.VMEM((tm, tf), _F32),
            pltpu.VMEM((tm, tf), _BF)]),
        compiler_params=_params("parallel", "arbitrary"),
        name="conv_ffn_ln",
    )(*args)
    if path.dense:
        conv_new = at.reshape(path.nb, path.t, dff)[:, path.t - (CONV_W - 1):]
    else:
        tps = path.t // tm
        conv_new = at.reshape(path.nb, tps, tail, dff)[:, tps - 1, tail - (CONV_W - 1):]
    return y, hn, conv_new


def _layer(x, h, path, mods, nxt, cache, w, alpha):
    rows, d = x.shape
    nb, t = path.nb, path.t
    sh1, sc1, gt1, sh2, sc2, gt2 = mods
    cos, sin = path.rope_tables()
    tm_big = path.tm(1024)

    q_cat, ckv, kr128 = _latent(h, w["w_lat"], w["g_q"], w["g_kv"], w["w_uq"], cos, sin, path)
    k_cat, v_a = _kvup(ckv, kr128, w["w_uk"], w["w_uv"])
    if cache is None:
        oa = _mla_prompt(q_cat, k_cat, v_a, nb, t)
    else:
        past = cache["ckv"].shape[1]
        kc, vc = _kvup(cache["ckv"].reshape(nb * past, KV_LORA),
                       jnp.pad(cache["krope"].reshape(nb * past, DR), ((0, 0), (0, LANE - DR))),
                       w["w_uk"], w["w_uv"])
        q_pos = (path.pos0 + np.arange(t))[:, None] // CHUNK
        ok1 = (np.arange(past)[None, :] // CHUNK) <= q_pos
        ok2 = ((past + np.arange(t))[None, :] // CHUNK) <= q_pos
        t1 = jnp.asarray(np.where(ok1, 0.0, NEG)[None], _F32)
        t2 = jnp.asarray(np.where(ok2, 0.0, NEG)[None], _F32)
        oa = _attn2(q_cat.reshape(nb, t, -1), kc.reshape(nb, past, -1), vc.reshape(nb, past, -1),
                    k_cat.reshape(nb, t, -1), v_a.reshape(nb, t, -1), t1, t2,
                    QK_CAT, DVA, V_CAT, None, True, 0, 0, 0, "mla_attention_cached")
        oa = oa.reshape(rows, HA * DVA)

    wb3 = HB * DHB
    qkv = _mm(h, w["w_band"], _BF, tm_big, _tile(3 * wb3, 1024, LANE), "band_qkv")
    if cache is None:
        ob = _band_prompt(qkv, w["rel_bias"], nb, t)
        keep = min(BAND_PREV * CHUNK, t)
        tk = _tile(keep, 512)
        per = keep // tk
        kv_new = _mm(h, w["w_band"][:, wb3:], _F32, tk, _tile(2 * wb3, 1024, LANE), "band_kv_tail",
                     row_map=lambda i: (i // per) * (t // tk) + (t - keep) // tk + i % per,
                     out_rows=nb * keep)
        bk_new = kv_new[:, :wb3].reshape(nb, keep, HB, DHB)
        bv_new = kv_new[:, wb3:].reshape(nb, keep, HB, DHB)
    else:
        kv_new = _mm(h, w["w_band"][:, wb3:], _F32, tm_big, _tile(2 * wb3, 1024, LANE), "band_kv_tail")
        bk_new = kv_new[:, :wb3].reshape(nb, t, HB, DHB)
        bv_new = kv_new[:, wb3:].reshape(nb, t, HB, DHB)
        nk = cache["band_k"].shape[1]
        t1 = _rel_table(w["rel_bias"], nk, t, nk, (path.pos0 - nk + np.arange(nk) >= 0)[None, :])
        t2 = _rel_table(w["rel_bias"], 0, t, t, np.ones((t, t), bool))
        qkv3 = qkv.reshape(nb, t, 3 * wb3)
        ob = _attn2(qkv3, cache["band_k"].reshape(nb, nk, wb3), cache["band_v"].reshape(nb, nk, wb3),
                    qkv3, qkv3, t1, t2, DHB, DHB, DHB, DHB ** -0.5, False, 0, 1, 2,
                    "band_attention_cached")
        ob = ob.reshape(rows, wb3)

    tm_r = path.tm(1024)
    qk_r = pl.pallas_call(
        _retqk_kernel,
        grid=(rows // tm_r,),
        in_specs=[_row_spec(tm_r, d), _full_spec(w["w_retqk"].shape),
                  path.rope_spec(cos, tm_r, 1), path.rope_spec(sin, tm_r, 1)],
        out_specs=_row_spec(tm_r, 2 * HR * DKR),
        out_shape=jax.ShapeDtypeStruct((rows, 2 * HR * DKR), _F32),
        compiler_params=_params("parallel"),
        name="ret_qk_rope",
    )(h, w["w_retqk"], cos, sin)
    v_r = _mm(h, w["w_retv"], _BF, tm_big, HR * DVR, "ret_v")
    g_r = _mm(h, w["w_retg"], _F32, tm_big, HR * DVR, "ret_gate")
    s0 = jnp.zeros((nb, HR, DKR, DVR), _F32) if cache is None else cache["ret"]
    oc, s_new = _retention(qk_r, v_r, g_r, s0, w["g_rn"], nb, t, 256)

    mg = _merge(h, oa, ob, oc, w["w_gates"], w["w_pa"], w["w_pb"], w["w_pc"], path)
    x1, h2 = _out_proj(mg, w["w_o"], x, gt1, w["ln1_g"], w["ln1_b"], sc2, sh2, path, alpha)

    prev = None if cache is None else cache["conv"]
    x2, hn, conv_new = _ffn(h2, x1, prev, w["w_fa"], w["w_fb"], w["cw"], w["cb"], w["w_fd"],
                            gt2, w["ln2_g"], w["ln2_b"], nxt[0], nxt[1], path, alpha)
    state = (ckv.reshape(nb, t, KV_LORA), kr128[:, :DR].reshape(nb, t, DR), bk_new, bv_new,
             s_new, conv_new)
    return x2, hn, state


def _layer_weights(l, wi, w_uq, w_ukv, w_ada_unused, p):
    d = wi.shape[1]
    o = np.cumsum([0, Q_LORA, KV_LORA, DR, HB * DHB, HB * DHB, HB * DHB, HR * DKR, HR * DKR,
                   HR * DVR, HR * DVR, d, d, d]).tolist()
    wl = wi[l]
    lat = jnp.concatenate([wl[:, :o[3]], jnp.zeros((d, LANE - DR), _BF)], axis=1)
    uq = w_uq[l].reshape(Q_LORA, HA, DN + DR)
    uq = jnp.pad(uq, ((0, 0), (0, 0), (0, QK_CAT - DN - DR))).reshape(Q_LORA, HA * QK_CAT)
    ukv = w_ukv[l].reshape(KV_LORA, HA, DN + DVA)
    row = lambda a: a[l][None, :]
    return {
        "w_lat": lat, "w_uq": uq,
        "w_uk": ukv[:, :, :DN].reshape(KV_LORA, HA * DN),
        "w_uv": ukv[:, :, DN:].reshape(KV_LORA, HA * DVA),
        "w_band": wl[:, o[3]:o[6]], "w_retqk": wl[:, o[6]:o[8]], "w_retv": wl[:, o[8]:o[9]],
        "w_retg": wl[:, o[9]:o[10]], "w_gates": wl[:, o[10]:o[13]],
        "g_q": row(p["g_q_lora"]), "g_kv": row(p["g_kv_lora"]), "rel_bias": p["rel_bias"][l],
        "g_rn": row(p["g_ret_norm"]),
        "w_pa": p["w_branch_a"][l], "w_pb": p["w_branch_b"][l], "w_pc": p["w_branch_c"][l],
        "w_o": p["w_o"][l], "ln1_g": row(p["ln1_g"]), "ln1_b": row(p["ln1_b"]),
        "w_fa": p["w_ff_a"][l], "w_fb": p["w_ff_b"][l], "cw": p["conv_w"][l],
        "cb": row(p["conv_b"]), "w_fd": p["w_ff_down"][l],
        "ln2_g": row(p["ln2_g"]), "ln2_b": row(p["ln2_b"]),
    }


def kernel(x_prompt, x_sample, c_prompt, c_sample, cache_mla_ckv, cache_mla_krope, cache_band_k, cache_band_v, state_ret, state_conv, w_ada, b_ada, w_in, g_q_lora, g_kv_lora, w_uq, w_ukv, rel_bias, g_ret_norm, w_branch_a, w_branch_b, w_branch_c, w_o, ln1_g, ln1_b, w_ff_a, w_ff_b, conv_w, conv_b, w_ff_down, ln2_g, ln2_b):
    nb_p, t_p, d = x_prompt.shape
    nb_s, t_s, _ = x_sample.shape
    depth = w_in.shape[0]
    past = cache_mla_ckv.shape[2]
    alpha = (2 * depth) ** 0.25

    nc = nb_p + nb_s
    ncp = -(-nc // 16) * 16
    c_all = jnp.pad(jnp.concatenate([c_prompt, c_sample], 0), ((0, ncp - nc), (0, 0)))
    tn_a = _tile(6 * d, 2048, LANE)
    ada = pl.pallas_call(
        _ada_kernel,
        grid=(depth, 6 * d // tn_a),
        in_specs=[pl.BlockSpec((ncp, d), lambda l, j: (0, 0)),
                  pl.BlockSpec((None, d, tn_a), lambda l, j: (l, 0, j)),
                  pl.BlockSpec((None, 1, tn_a), lambda l, j: (l, 0, j))],
        out_specs=pl.BlockSpec((None, ncp, tn_a), lambda l, j: (l, 0, j)),
        out_shape=jax.ShapeDtypeStruct((depth, ncp, 6 * d), _F32),
        compiler_params=_params("parallel", "arbitrary"),
        name="ada_ln",
    )(c_all, w_ada.astype(_BF), b_ada[:, None, :])

    def mods(l, lo, n):
        a = ada[l, lo:lo + n]
        sh1, sc1, gt1, sh2, sc2, gt2 = [a[:, k * d:(k + 1) * d] for k in range(6)]
        return (sh1, 1.0 + sc1, 1.0 + gt1, sh2, 1.0 + sc2, 1.0 + gt2)

    bf = lambda a: a.astype(_BF)
    p = dict(g_q_lora=g_q_lora, g_kv_lora=g_kv_lora, rel_bias=rel_bias, g_ret_norm=g_ret_norm,
             w_branch_a=bf(w_branch_a), w_branch_b=bf(w_branch_b), w_branch_c=bf(w_branch_c),
             w_o=bf(w_o), ln1_g=ln1_g, ln1_b=ln1_b, w_ff_a=bf(w_ff_a), w_ff_b=bf(w_ff_b),
             conv_w=conv_w, conv_b=conv_b, w_ff_down=bf(w_ff_down), ln2_g=ln2_g, ln2_b=ln2_b)
    wi, wq, wkv = bf(w_in), bf(w_uq), bf(w_ukv)

    streams = [(_Path(nb_p, t_p, 0, False), x_prompt, 0),
               (_Path(nb_s, t_s, past, True), x_sample, nb_p)]
    outs = []
    for path, x0, lo in streams:
        x = x0.reshape(path.rows, d)
        m0 = mods(0, lo, path.nb)
        h = _modulate(x, m0[1], m0[0], path)
        states = []
        for l in range(depth):
            m = mods(l, lo, path.nb)
            mn = mods(min(l + 1, depth - 1), lo, path.nb)
            cache = None
            if path.dense:
                cache = dict(ckv=cache_mla_ckv[l], krope=cache_mla_krope[l], band_k=cache_band_k[l],
                             band_v=cache_band_v[l], ret=state_ret[l], conv=state_conv[l])
            w = _layer_weights(l, wi, wq, wkv, None, p)
            x, h, st = _layer(x, h, path, m, (mn[1], mn[0]), cache, w, alpha)
            states.append(st)
        outs.append((x.reshape(path.nb, path.t, d), [jnp.stack(z) for z in zip(*states)]))
    (y_p, st_p), (y_s, st_s) = outs
    return (y_p, y_s, *st_p, *st_s)
```

```python
import functools

import numpy as np
import jax
import jax.numpy as jnp
from jax import lax
from jax.experimental import pallas as pl
from jax.experimental.pallas import tpu as pltpu

CHUNK = 64
HA, DN, DR, DVA = 8, 128, 64, 128
Q_LORA, KV_LORA = 512, 256
HB, DHB = 8, 128
BAND_PREV, REL_CLIP = 8, 128
HR, DKR, DVR = 8, 64, 128
CONV_W = 3
ROPE_THETA = 10000.0
EPS = 1e-5
LANE = 128
QK_CAT = DN + 2 * DR
NEG = -1e30
VMEM_LIMIT = 56 * 1024 * 1024
MLA_QSCALE = (DN + DR) ** -0.5 * float(np.log2(np.e))
V_CAT = 2 * DVA
FFN_HALO = 8
FFN_ROWS = 64
FFN_GROUPS = 2

_BF = jnp.bfloat16
_F32 = jnp.float32


def _dot(a, b):
    return jnp.dot(a, b, preferred_element_type=_F32)


def _dot_nt(a, b):
    return lax.dot_general(a, b, (((1,), (1,)), ((), ())), preferred_element_type=_F32)


def _dot_tn(a, b):
    return lax.dot_general(a, b, (((0,), (0,)), ((), ())), preferred_element_type=_F32)


def _tile(n, pref, step=8):
    if n <= pref:
        return n
    t = pref - pref % step
    while n % t:
        t -= step
    return t


def _params(*sem):
    return pltpu.CompilerParams(dimension_semantics=sem, vmem_limit_bytes=VMEM_LIMIT)


def _swap32(x):
    n = x.shape[-1]
    lane = lax.broadcasted_iota(jnp.int32, x.shape, 1)
    fwd = pltpu.roll(x, n - DR // 2, 1)
    bwd = pltpu.roll(x, DR // 2, 1)
    return jnp.where((lane % DR) < DR // 2, fwd, bwd)


def _rope128(x, cos, sin):
    return x * cos + _swap32(x) * sin


def _layer_norm(z, g, b):
    mu = jnp.mean(z, -1, keepdims=True)
    zc = z - mu
    var = jnp.mean(zc * zc, -1, keepdims=True)
    return zc * lax.rsqrt(var + EPS) * g + b


def _rms(x, g):
    return x * lax.rsqrt(jnp.mean(x * x, -1, keepdims=True) + EPS) * g


def _ada_kernel(c_ref, w_ref, b_ref, o_ref):
    c = c_ref[...]
    s = (c * jax.nn.sigmoid(c)).astype(_BF)
    o_ref[...] = _dot(s, w_ref[...]) + b_ref[...]


def _mod_kernel(x_ref, sc_ref, sh_ref, o_ref):
    o_ref[...] = (x_ref[...] * sc_ref[...] + sh_ref[...]).astype(o_ref.dtype)


def _mm_kernel(x_ref, w_ref, o_ref):
    o_ref[...] = _dot(x_ref[...], w_ref[...]).astype(o_ref.dtype)


def _retqk_kernel(x_ref, w_ref, cos_ref, sin_ref, o_ref):
    y = _dot(x_ref[...], w_ref[...])
    cos, sin = cos_ref[...], sin_ref[...]
    nq = HR * DKR // LANE
    for c in range(2 * nq):
        blk = _rope128(y[:, c * LANE:(c + 1) * LANE], cos, sin)
        if c < nq:
            blk = blk * (DKR ** -0.5)
        o_ref[:, c * LANE:(c + 1) * LANE] = blk


def _lat_kernel(x_ref, wl_ref, gq_ref, gkv_ref, wuq_ref, cos_ref, sin_ref,
                q_ref, ckv_ref, kr_ref):
    y = _dot(x_ref[...], wl_ref[...])
    cos, sin = cos_ref[...], sin_ref[...]
    cqn = _rms(y[:, :Q_LORA], gq_ref[...])
    ckv_ref[...] = _rms(y[:, Q_LORA:Q_LORA + KV_LORA], gkv_ref[...])
    kr_ref[...] = _rope128(y[:, Q_LORA + KV_LORA:], cos, sin)
    qa = _dot(cqn.astype(_BF), wuq_ref[...]) * MLA_QSCALE
    for h in range(HA):
        o = h * QK_CAT
        q_ref[:, o:o + DN] = qa[:, o:o + DN].astype(_BF)
        q_ref[:, o + DN:o + QK_CAT] = _rope128(qa[:, o + DN:o + QK_CAT], cos, sin).astype(_BF)


def _kvup_kernel(ckv_ref, kr_ref, wk_ref, wv_ref, k_ref, v_ref):
    c = ckv_ref[...].astype(_BF)
    kn = _dot(c, wk_ref[...])
    vn = _dot(c, wv_ref[...])
    krb = kr_ref[...].astype(_BF)
    ones = jnp.ones((c.shape[0], DVA), _BF)
    for h in range(HA):
        o = h * QK_CAT
        k_ref[:, o:o + DN] = kn[:, h * DN:(h + 1) * DN].astype(_BF)
        k_ref[:, o + DN:o + QK_CAT] = krb
        v_ref[:, h * V_CAT:h * V_CAT + DVA] = vn[:, h * DVA:(h + 1) * DVA].astype(_BF)
        v_ref[:, h * V_CAT + DVA:(h + 1) * V_CAT] = ones


def _kvup_t_kernel(ckv_ref, kr_ref, wkt_ref, wv_ref, kt_ref, v_ref):
    c = ckv_ref[...].astype(_BF)
    knt = _dot_nt(wkt_ref[...], c)
    vn = _dot(c, wv_ref[...])
    krt = kr_ref[...].T.astype(_BF)
    ones = jnp.ones((c.shape[0], DVA), _BF)
    for h in range(HA):
        o = h * QK_CAT
        kt_ref[o:o + DN, :] = knt[h * DN:(h + 1) * DN, :].astype(_BF)
        kt_ref[o + DN:o + QK_CAT, :] = krt
        v_ref[:, h * V_CAT:h * V_CAT + DVA] = vn[:, h * DVA:(h + 1) * DVA].astype(_BF)
        v_ref[:, h * V_CAT + DVA:(h + 1) * V_CAT] = ones


def _mla_kernel(q_ref, kt_ref, v_ref, o_ref, s_scr, p_scr, m_scr, a_scr, acc_scr, *, hg, tq, tk):
    qi = pl.program_id(2)
    ratio = tq // tk
    kc = tk // CHUNK
    m_scr[...] = jnp.full(m_scr.shape, NEG, _F32)
    acc_scr[...] = jnp.zeros(acc_scr.shape, _F32)
    col = lax.broadcasted_iota(jnp.int32, (CHUNK, tk), 1)

    def tile(kt, diag):
        start = pl.multiple_of(kt * tk, tk)
        for h in range(hg):
            s_scr[h] = _dot(q_ref[:, h * QK_CAT:(h + 1) * QK_CAT],
                            kt_ref[kt, h * QK_CAT:(h + 1) * QK_CAT, :])
        for h in range(hg):
            for r in range(tq // CHUNK):
                rows = slice(r * CHUNK, (r + 1) * CHUNK)
                ncol = tk if diag is None else min(max((r - diag * kc + 1) * CHUNK, 0), tk)
                if ncol == 0:
                    p_scr[h, rows, :] = jnp.zeros((CHUNK, tk), _BF)
                    a_scr[h, rows, :] = jnp.ones((CHUNK, 1), _F32)
                    continue
                s = s_scr[h, rows, :]
                if ncol < tk:
                    s = jnp.where(col < ncol, s, NEG)
                m_old = m_scr[h, rows, :]
                m_new = jnp.maximum(m_old, jnp.max(s, -1, keepdims=True))
                p_scr[h, rows, :] = jnp.exp2(s - m_new).astype(_BF)
                a_scr[h, rows, :] = jnp.exp2(m_old - m_new)
                m_scr[h, rows, :] = m_new
            acc_scr[h] = a_scr[h] * acc_scr[h] + _dot(
                p_scr[h], v_ref[pl.ds(start, tk), h * V_CAT:(h + 1) * V_CAT])

    def body(kt, c):
        tile(kt, None)
        return c

    lax.fori_loop(0, qi * ratio, body, 0)
    for d in range(ratio):
        tile(qi * ratio + d, d)
    for h in range(hg):
        acc = acc_scr[h]
        o_ref[:, h * DVA:(h + 1) * DVA] = (acc[:, :DVA] / acc[:, DVA:]).astype(o_ref.dtype)


def _band_kernel(q_ref, k_ref, v_ref, tb_ref, o_ref, *, hg, win, chunks_per_tile, scale):
    n = pl.program_id(2)
    ws = pl.multiple_of(jnp.maximum(n * chunks_per_tile - BAND_PREV, 0) * CHUNK, CHUNK)
    for h in range(hg):
        q = q_ref[:, h * DHB:(h + 1) * DHB]
        k = k_ref[pl.ds(ws, win), h * DHB:(h + 1) * DHB]
        v = v_ref[pl.ds(ws, win), h * DHB:(h + 1) * DHB]
        s = _dot_nt(q, k) * scale + tb_ref[h]
        m = jnp.max(s, -1, keepdims=True)
        p = jnp.exp(s - m)
        l = jnp.sum(p, -1, keepdims=True)
        o_ref[:, h * DHB:(h + 1) * DHB] = (_dot(p.astype(_BF), v) / l).astype(o_ref.dtype)


def _attn2_kernel(q_ref, k1_ref, v1_ref, k2_ref, v2_ref, t1_ref, t2_ref, o_ref,
                  *, heads, dq, dv, v_stride, scale, base2, per_head_table):
    ex = jnp.exp2 if base2 else jnp.exp
    for h in range(heads):
        th = h if per_head_table else 0
        q = q_ref[:, h * dq:(h + 1) * dq]
        k1 = k1_ref[:, h * dq:(h + 1) * dq].astype(_BF)
        k2 = k2_ref[:, h * dq:(h + 1) * dq].astype(_BF)
        v1 = v1_ref[:, h * v_stride:h * v_stride + dv].astype(_BF)
        v2 = v2_ref[:, h * v_stride:h * v_stride + dv].astype(_BF)
        s1, s2 = _dot_nt(q, k1), _dot_nt(q, k2)
        if scale is not None:
            s1, s2 = s1 * scale, s2 * scale
        s1, s2 = s1 + t1_ref[th], s2 + t2_ref[th]
        m = jnp.maximum(jnp.max(s1, -1, keepdims=True), jnp.max(s2, -1, keepdims=True))
        p1 = ex(s1 - m)
        p2 = ex(s2 - m)
        l = jnp.sum(p1, -1, keepdims=True) + jnp.sum(p2, -1, keepdims=True)
        o = _dot(p1.astype(_BF), v1) + _dot(p2.astype(_BF), v2)
        o_ref[:, h * dv:(h + 1) * dv] = (o / l).astype(o_ref.dtype)


def _ret_kernel(qk_ref, v_ref, gr_ref, s0_ref, dq_ref, dk_ref, dm_ref, ds_ref, grn_ref,
                o_ref, sout_ref, s_scr):
    ci = pl.program_id(1)
    npair = HR // 2
    r2 = lax.broadcasted_iota(jnp.int32, (2 * DKR, 2 * DVR), 0) // DKR
    c2 = lax.broadcasted_iota(jnp.int32, (2 * DKR, 2 * DVR), 1) // DVR
    on_diag = r2 == c2

    @pl.when(ci == 0)
    def _():
        for p in range(npair):
            top = jnp.concatenate([s0_ref[2 * p], jnp.zeros((DKR, DVR), _F32)], axis=1)
            bot = jnp.concatenate([jnp.zeros((DKR, DVR), _F32), s0_ref[2 * p + 1]], axis=1)
            s_scr[p] = jnp.concatenate([top, bot], axis=0)

    lane = lax.broadcasted_iota(jnp.int32, (qk_ref.shape[0], LANE), 1)
    for p in range(npair):
        q2 = qk_ref[:, p * LANE:(p + 1) * LANE]
        k2 = qk_ref[:, HR * DKR + p * LANE:HR * DKR + (p + 1) * LANE]
        v2 = v_ref[:, 2 * p * DVR:(2 * p + 2) * DVR]
        k2b = k2.astype(_BF)
        s2 = s_scr[p]
        cross = _dot((q2 * dq_ref[p]).astype(_BF), s2.astype(_BF))
        for e in range(2):
            h = 2 * p + e
            qm = jnp.where((lane // DKR) == e, q2, 0.0).astype(_BF)
            inner = _dot_nt(qm, k2b) * dm_ref[h]
            o = _dot(inner.astype(_BF), v2[:, e * DVR:(e + 1) * DVR])
            o = o + cross[:, e * DVR:(e + 1) * DVR]
            mu = jnp.mean(o, -1, keepdims=True)
            oc = o - mu
            var = jnp.mean(oc * oc, -1, keepdims=True)
            on = oc * lax.rsqrt(var + EPS) * grn_ref[:, h * DVR:(h + 1) * DVR]
            g = gr_ref[:, h * DVR:(h + 1) * DVR]
            o_ref[:, h * DVR:(h + 1) * DVR] = (g * jax.nn.sigmoid(g) * on).astype(o_ref.dtype)
        upd = _dot_tn((k2 * dk_ref[p]).astype(_BF), v2)
        s_scr[p] = s2 * ds_ref[p] + jnp.where(on_diag, upd, 0.0)

    @pl.when(ci == pl.num_programs(1) - 1)
    def _():
        for p in range(npair):
            sp = s_scr[p]
            sout_ref[2 * p] = sp[:DKR, :DVR]
            sout_ref[2 * p + 1] = sp[DKR:, DVR:]


def _merge_kernel(h_ref, oa_ref, ob_ref, oc_ref, wga_ref, wgb_ref, wgc_ref,
                  wpa_ref, wpb_ref, wpc_ref, o_ref):
    h = h_ref[...]

    def branch(o_r, wg_r, wp_r):
        return jax.nn.sigmoid(_dot(h, wg_r[...])) * _dot(o_r[...], wp_r[...])

    m = branch(oa_ref, wga_ref, wpa_ref) + branch(ob_ref, wgb_ref, wpb_ref)
    m = m + branch(oc_ref, wgc_ref, wpc_ref)
    o_ref[...] = m.astype(o_ref.dtype)


def _out_kernel(mg_ref, wo_ref, x_ref, gt_ref, lng_ref, lnb_ref, sc_ref, sh_ref,
                x1_ref, h2_ref, *, alpha):
    y = _dot(mg_ref[...], wo_ref[...])
    x1 = _layer_norm(alpha * x_ref[...] + gt_ref[...] * y, lng_ref[...], lnb_ref[...])
    x1_ref[...] = x1
    h2_ref[...] = (x1 * sc_ref[...] + sh_ref[...]).astype(h2_ref.dtype)


def _gated(a, a1, a2, bb, cw, cb):
    conv = cb + a2 * cw[0:1, :] + a1 * cw[1:2, :] + a * cw[2:3, :]
    gelu = 0.5 * conv * (1.0 + lax.erf(conv * (2.0 ** -0.5)))
    return (gelu * bb).astype(_BF)


def _ffn_clear(acc_ref):
    @pl.when(pl.program_id(1) == 0)
    def _():
        acc_ref[...] = jnp.zeros(acc_ref.shape, _F32)


def _ffn_norm(x_ref, gt_ref, lng_ref, lnb_ref, sc_ref, sh_ref, y_ref, hn_ref, acc_ref, alpha):
    @pl.when(pl.program_id(1) == pl.num_programs(1) - 1)
    def _():
        z = alpha * x_ref[...] + gt_ref[...] * acc_ref[...]
        y = _layer_norm(z, lng_ref[...], lnb_ref[...])
        y_ref[...] = y
        hn_ref[...] = (y * sc_ref[...] + sh_ref[...]).astype(hn_ref.dtype)


def _ffn_halo_kernel(h_ref, halo_ref, x_ref, wa_ref, wb_ref, cw_ref, cb_ref, wd_ref, gt_ref,
                     lng_ref, lnb_ref, sc_ref, sh_ref, y_ref, hn_ref, at_ref, acc_ref,
                     *scr, tiles_per_seq, alpha):
    ng = len(scr) // 3
    a_scr, b_scr, g_scr = scr[:ng], scr[ng:2 * ng], scr[2 * ng:]
    i = pl.program_id(0)
    h = h_ref[...]
    halo = halo_ref[...]
    tm = h.shape[0]
    w = wa_ref.shape[1] // ng
    keep = jnp.where(i % tiles_per_seq == 0, 0.0, 1.0)
    _ffn_clear(acc_ref)
    for c in range(ng):
        cs = slice(c * w, (c + 1) * w)
        a_scr[c][FFN_HALO:, :] = _dot(h, wa_ref[:, cs])
        b_scr[c][...] = _dot(h, wb_ref[:, cs])
        ah = _dot(halo, wa_ref[:, cs])
        a_scr[c][:FFN_HALO, :] = ah[ah.shape[0] - FFN_HALO:, :] * keep
        at_ref[:, cs] = a_scr[c][tm:, :]
    rc = min(FFN_ROWS, tm)
    for c in range(ng):
        cs = slice(c * w, (c + 1) * w)
        cw, cb = cw_ref[:, cs], cb_ref[:, cs]
        for k in range(tm // rc):
            r = FFN_HALO + k * rc
            g_scr[c][r - FFN_HALO:r - FFN_HALO + rc, :] = _gated(
                a_scr[c][r:r + rc, :], a_scr[c][r - 1:r - 1 + rc, :], a_scr[c][r - 2:r - 2 + rc, :],
                b_scr[c][r - FFN_HALO:r - FFN_HALO + rc, :], cw, cb)
        acc_ref[...] += _dot(g_scr[c][...], wd_ref[cs, :])
    _ffn_norm(x_ref, gt_ref, lng_ref, lnb_ref, sc_ref, sh_ref, y_ref, hn_ref, acc_ref, alpha)


def _ffn_seg_kernel(h_ref, p1_ref, p2_ref, x_ref, wa_ref, wb_ref, cw_ref, cb_ref, wd_ref,
                    gt_ref, lng_ref, lnb_ref, sc_ref, sh_ref, y_ref, hn_ref, at_ref, acc_ref,
                    *, seg, alpha):
    h = h_ref[...]
    a = _dot(h, wa_ref[...])
    bb = _dot(h, wb_ref[...])
    rs = lax.broadcasted_iota(jnp.int32, a.shape, 0) % seg
    a1 = jnp.where(rs == 0, p1_ref[...], pltpu.roll(a, 1, 0))
    a2 = jnp.where(rs < 2, p2_ref[...], pltpu.roll(a, 2, 0))
    at_ref[...] = a
    _ffn_clear(acc_ref)
    acc_ref[...] += _dot(_gated(a, a1, a2, bb, cw_ref[...], cb_ref[...]), wd_ref[...])
    _ffn_norm(x_ref, gt_ref, lng_ref, lnb_ref, sc_ref, sh_ref, y_ref, hn_ref, acc_ref, alpha)


class _Path:
    def __init__(self, nb, t, pos0, dense):
        self.nb, self.t, self.rows, self.pos0, self.dense = nb, t, nb * t, pos0, dense

    def tm(self, pref):
        return _tile(self.rows if self.dense else self.t, pref)

    def mod(self, v, tm):
        d = v.shape[-1]
        if self.dense:
            return jnp.repeat(v, self.t, axis=0).reshape(self.rows // tm, tm, d)
        return v[:, None, :]

    def mod_spec(self, arr, tm, rank):
        g, r, d = arr.shape
        tpg = (self.rows // tm) // g
        if rank == 1:
            return pl.BlockSpec((None, r, d), lambda i: (i // tpg, 0, 0))
        return pl.BlockSpec((None, r, d), lambda i, j: (i // tpg, 0, 0))

    def rope_tables(self):
        half = DR // 2
        inv = ROPE_THETA ** (-jnp.arange(half, dtype=_F32) / half)
        pos = self.pos0 + jnp.arange(self.t, dtype=jnp.int32)
        ang = pos.astype(_F32)[:, None] * inv[None, :]
        cos, sin = jnp.cos(ang), jnp.sin(ang)
        cos = jnp.concatenate([cos, cos, cos, cos], -1)
        sin = jnp.concatenate([-sin, sin, -sin, sin], -1)
        if self.dense:
            cos, sin = jnp.tile(cos, (self.nb, 1)), jnp.tile(sin, (self.nb, 1))
        return cos, sin

    def rope_spec(self, tab, tm, rank):
        nt = tab.shape[0] // tm
        if rank == 1:
            return pl.BlockSpec((tm, LANE), lambda i: (i % nt, 0))
        return pl.BlockSpec((tm, LANE), lambda i, j: (i % nt, 0))


def _row_spec(tm, n):
    return pl.BlockSpec((tm, n), lambda i: (i, 0))


def _full_spec(shape):
    nd = len(shape)
    return pl.BlockSpec(shape, lambda *_: (0,) * nd)


def _mm(x, w, out_dtype, tm, tn, name, row_map=None, out_rows=None):
    rows, k = x.shape
    n = w.shape[1]
    out_rows = rows if out_rows is None else out_rows
    row_map = (lambda i: i) if row_map is None else row_map
    return pl.pallas_call(
        _mm_kernel,
        grid=(out_rows // tm, n // tn),
        in_specs=[pl.BlockSpec((tm, k), lambda i, j: (row_map(i), 0)),
                  pl.BlockSpec((k, tn), lambda i, j: (0, j))],
        out_specs=pl.BlockSpec((tm, tn), lambda i, j: (i, j)),
        out_shape=jax.ShapeDtypeStruct((out_rows, n), out_dtype),
        compiler_params=_params("parallel", "arbitrary"),
        name=name,
    )(x, w)


def _modulate(x, sc, sh, path):
    rows, d = x.shape
    tm = path.tm(1024)
    sc, sh = path.mod(sc, tm), path.mod(sh, tm)
    return pl.pallas_call(
        _mod_kernel,
        grid=(rows // tm,),
        in_specs=[_row_spec(tm, d), path.mod_spec(sc, tm, 1), path.mod_spec(sh, tm, 1)],
        out_specs=_row_spec(tm, d),
        out_shape=jax.ShapeDtypeStruct((rows, d), _BF),
        compiler_params=_params("parallel"),
        name="modulate",
    )(x, sc, sh)


def _latent(h, wl, gq, gkv, wuq, cos, sin, path):
    rows, d = h.shape
    tm = path.tm(512)
    nl = wl.shape[1]
    return pl.pallas_call(
        _lat_kernel,
        grid=(rows // tm,),
        in_specs=[_row_spec(tm, d), _full_spec(wl.shape), _full_spec(gq.shape),
                  _full_spec(gkv.shape), _full_spec(wuq.shape),
                  path.rope_spec(cos, tm, 1), path.rope_spec(sin, tm, 1)],
        out_specs=[_row_spec(tm, HA * QK_CAT), _row_spec(tm, KV_LORA), _row_spec(tm, LANE)],
        out_shape=[jax.ShapeDtypeStruct((rows, HA * QK_CAT), _BF),
                   jax.ShapeDtypeStruct((rows, KV_LORA), _F32),
                   jax.ShapeDtypeStruct((rows, LANE), _F32)],
        compiler_params=_params("parallel"),
        name="mla_latent",
    )(h, wl, gq, gkv, wuq, cos, sin)


def _kvup(ckv, kr, wk, wv, rows=None, row_off=0):
    rows = ckv.shape[0] if rows is None else rows
    tm = _tile(rows, 1024)
    off = row_off // tm
    return pl.pallas_call(
        _kvup_kernel,
        grid=(rows // tm,),
        in_specs=[pl.BlockSpec((tm, KV_LORA), lambda i: (off + i, 0)),
                  pl.BlockSpec((tm, LANE), lambda i: (off + i, 0)), _full_spec(wk.shape),
                  _full_spec(wv.shape)],
        out_specs=[_row_spec(tm, HA * QK_CAT), _row_spec(tm, HA * V_CAT)],
        out_shape=[jax.ShapeDtypeStruct((rows, HA * QK_CAT), _BF),
                   jax.ShapeDtypeStruct((rows, HA * V_CAT), _BF)],
        compiler_params=_params("parallel"),
        name="mla_kv_up",
    )(ckv, kr, wk, wv)


def _kvup_t(ckv, kr, wkt, wv, tk):
    rows = ckv.shape[0]
    return pl.pallas_call(
        _kvup_t_kernel,
        grid=(rows // tk,),
        in_specs=[_row_spec(tk, KV_LORA), _row_spec(tk, LANE), _full_spec(wkt.shape),
                  _full_spec(wv.shape)],
        out_specs=[pl.BlockSpec((None, HA * QK_CAT, tk), lambda i: (i, 0, 0)),
                   _row_spec(tk, HA * V_CAT)],
        out_shape=[jax.ShapeDtypeStruct((rows // tk, HA * QK_CAT, tk), _BF),
                   jax.ShapeDtypeStruct((rows, HA * V_CAT), _BF)],
        compiler_params=_params("parallel"),
        name="mla_kv_up_t",
    )(ckv, kr, wkt, wv)


def _mla_tiles(t):
    tk = _tile(t, 512, CHUNK)
    tq = 2 * tk if t % (2 * tk) == 0 else tk
    return tq, tk


def _mla_prompt(q, kt, v, nb, t):
    hg = 2
    tq, tk = _mla_tiles(t)
    nq, nk = t // tq, t // tk
    kern = functools.partial(_mla_kernel, hg=hg, tq=tq, tk=tk)
    return pl.pallas_call(
        kern,
        grid=(nb, HA // hg, nq),
        in_specs=[pl.BlockSpec((tq, hg * QK_CAT), lambda b, g, i: (b * nq + i, g)),
                  pl.BlockSpec((nk, hg * QK_CAT, tk), lambda b, g, i: (b, g, 0)),
                  pl.BlockSpec((t, hg * V_CAT), lambda b, g, i: (b, g))],
        out_specs=pl.BlockSpec((tq, hg * DVA), lambda b, g, i: (b * nq + i, g)),
        out_shape=jax.ShapeDtypeStruct((nb * t, HA * DVA), _BF),
        scratch_shapes=[pltpu.VMEM((hg, tq, tk), _F32), pltpu.VMEM((hg, tq, tk), _BF),
                        pltpu.VMEM((hg, tq, 1), _F32), pltpu.VMEM((hg, tq, 1), _F32),
                        pltpu.VMEM((hg, tq, V_CAT), _F32)],
        compiler_params=_params("parallel", "parallel", "arbitrary"),
        name="mla_attention",
    )(q, kt, v)


def _band_tables(rel_bias, tq, t):
    r = tq // CHUNK
    win = (r + BAND_PREV) * CHUNK
    nv = -(-BAND_PREV // r) + 1
    i = np.arange(tq)[:, None]
    m = np.arange(win)[None, :]
    tabs = []
    for var in range(nv):
        n0 = var * r
        ws = max(n0 - BAND_PREV, 0)
        cq = n0 + i // CHUNK
        ck = ws + m // CHUNK
        allowed = (ck <= cq) & (ck >= cq - BAND_PREV)
        tabs.append(_rel_table(rel_bias, (n0 - ws) * CHUNK, tq, win, allowed))
    return jnp.stack(tabs), win, nv


def _rel_table(rel_bias, off, nq, nk, allowed):
    period = nq + nk
    u = np.arange(period)
    u = np.where(u < nk, u, u - period)
    f = rel_bias[:, np.clip(off - u, -REL_CLIP, REL_CLIP) + REL_CLIP]
    circ = jnp.tile(f, (1, nq))[:, :nq * (period - 1)].reshape(-1, nq, period - 1)[:, :, :nk]
    return jnp.where(jnp.asarray(np.broadcast_to(allowed, (nq, nk)))[None], circ, NEG).astype(_F32)


def _band_prompt(qkv, rel_bias, nb, t):
    hg = 4
    tq = _tile(t, 256, CHUNK)
    nq = t // tq
    tabs, win, nv = _band_tables(rel_bias, tq, t)
    ng = HB // hg
    kern = functools.partial(_band_kernel, hg=hg, win=win, chunks_per_tile=tq // CHUNK,
                             scale=DHB ** -0.5)
    return pl.pallas_call(
        kern,
        grid=(nb, ng, nq),
        in_specs=[pl.BlockSpec((tq, hg * DHB), lambda b, g, i: (b * nq + i, g)),
                  pl.BlockSpec((t, hg * DHB), lambda b, g, i: (b, ng + g)),
                  pl.BlockSpec((t, hg * DHB), lambda b, g, i: (b, 2 * ng + g)),
                  pl.BlockSpec((None, hg, tq, win),
                               lambda b, g, i: (jnp.minimum(i, nv - 1), g, 0, 0))],
        out_specs=pl.BlockSpec((tq, hg * DHB), lambda b, g, i: (b * nq + i, g)),
        out_shape=jax.ShapeDtypeStruct((nb * t, HB * DHB), _BF),
        compiler_params=_params("parallel", "parallel", "arbitrary"),
        name="band_attention",
    )(qkv, qkv, qkv, tabs)


def _attn2(q, k1, v1, k2, v2, t1, t2, dq, dv, v_stride, scale, base2, qoff, koff, voff, name,
           layer=None):
    nb, tq, _ = q.shape
    n1 = k1.shape[-2]
    heads = HA
    kern = functools.partial(_attn2_kernel, heads=heads, dq=dq, dv=dv, v_stride=v_stride,
                             scale=scale, base2=base2, per_head_table=t1.shape[0] > 1)

    def cached(width):
        if layer is None:
            return pl.BlockSpec((None, n1, width), lambda b: (b, 0, 0))
        return pl.BlockSpec((None, None, n1, width), lambda b: (layer, b, 0, 0))

    return pl.pallas_call(
        kern,
        grid=(nb,),
        in_specs=[pl.BlockSpec((None, tq, heads * dq), lambda b: (b, 0, qoff)),
                  cached(heads * dq), cached(heads * v_stride),
                  pl.BlockSpec((None, tq, heads * dq), lambda b: (b, 0, koff)),
                  pl.BlockSpec((None, tq, heads * v_stride), lambda b: (b, 0, voff)),
                  _full_spec(t1.shape), _full_spec(t2.shape)],
        out_specs=pl.BlockSpec((None, tq, heads * dv), lambda b: (b, 0, 0)),
        out_shape=jax.ShapeDtypeStruct((nb, tq, heads * dv), _BF),
        compiler_params=_params("parallel"),
        name=name,
    )(q, k1, v1, k2, v2, t1, t2)


def _ret_tables(lc):
    lg = np.log1p(-np.exp2(-5.0 - np.arange(HR, dtype=np.float64)))
    idx = np.arange(lc, dtype=np.float64)
    lane_head = np.arange(LANE) // DKR
    npair = HR // 2
    dq = np.stack([np.exp((idx[:, None] + 1.0) * lg[2 * p + lane_head][None, :])
                   for p in range(npair)])
    dk = np.stack([np.exp((lc - 1.0 - idx)[:, None] * lg[2 * p + lane_head][None, :])
                   for p in range(npair)])
    diff = idx[:, None] - idx[None, :]
    dm = np.stack([np.where(diff >= 0, np.exp(np.maximum(diff, 0.0) * lg[h]), 0.0)
                   for h in range(HR)])
    rh = np.arange(2 * DKR)[:, None] // DKR
    ch = np.arange(2 * DVR)[None, :] // DVR
    ds = np.stack([np.where(rh == ch, np.exp(lc * lg[2 * p + rh]) + 0.0 * ch, 0.0)
                   for p in range(npair)])
    return [jnp.asarray(a, _F32) for a in (dq, dk, dm, ds)]


def _retention(qk, v, gr, s0, grn, nb, t, lc_pref):
    lc = _tile(t, lc_pref, CHUNK) if t > CHUNK else t
    nc = t // lc
    dq, dk, dm, ds = _ret_tables(lc)
    w = HR * DVR
    return pl.pallas_call(
        _ret_kernel,
        grid=(nb, nc),
        in_specs=[pl.BlockSpec((lc, 2 * HR * DKR), lambda b, c: (b * nc + c, 0)),
                  pl.BlockSpec((lc, w), lambda b, c: (b * nc + c, 0)),
                  pl.BlockSpec((lc, w), lambda b, c: (b * nc + c, 0)),
                  pl.BlockSpec((None, HR, DKR, DVR), lambda b, c: (b, 0, 0, 0)),
                  _full_spec(dq.shape), _full_spec(dk.shape), _full_spec(dm.shape),
                  _full_spec(ds.shape), _full_spec(grn.shape)],
        out_specs=[pl.BlockSpec((lc, w), lambda b, c: (b * nc + c, 0)),
                   pl.BlockSpec((None, HR, DKR, DVR), lambda b, c: (b, 0, 0, 0))],
        out_shape=[jax.ShapeDtypeStruct((nb * t, w), _BF),
                   jax.ShapeDtypeStruct((nb, HR, DKR, DVR), _F32)],
        scratch_shapes=[pltpu.VMEM((HR // 2, 2 * DKR, 2 * DVR), _F32)],
        compiler_params=_params("parallel", "arbitrary"),
        name="retention",
    )(qk, v, gr, s0, dq, dk, dm, ds, grn)


def _merge(h, oa, ob, oc, wg, wpa, wpb, wpc, path):
    rows, d = h.shape
    tm = path.tm(512)
    tn = _tile(d, 512, LANE)
    nj = d // tn
    wo = oa.shape[1]
    o_spec = pl.BlockSpec((tm, wo), lambda i, j: (i, 0))
    wp_spec = pl.BlockSpec((wo, tn), lambda i, j: (0, j))
    return pl.pallas_call(
        _merge_kernel,
        grid=(rows // tm, nj),
        in_specs=[pl.BlockSpec((tm, d), lambda i, j: (i, 0)), o_spec, o_spec, o_spec,
                  pl.BlockSpec((d, tn), lambda i, j: (0, j)),
                  pl.BlockSpec((d, tn), lambda i, j: (0, nj + j)),
                  pl.BlockSpec((d, tn), lambda i, j: (0, 2 * nj + j)),
                  wp_spec, wp_spec, wp_spec],
        out_specs=pl.BlockSpec((tm, tn), lambda i, j: (i, j)),
        out_shape=jax.ShapeDtypeStruct((rows, d), _BF),
        compiler_params=_params("parallel", "arbitrary"),
        name="gated_merge",
    )(h, oa, ob, oc, wg, wg, wg, wpa, wpb, wpc)


def _out_proj(mg, wo, x, gt, lng, lnb, sc, sh, path, alpha):
    rows, d = x.shape
    tm = path.tm(512)
    gt, sc, sh = path.mod(gt, tm), path.mod(sc, tm), path.mod(sh, tm)
    return pl.pallas_call(
        functools.partial(_out_kernel, alpha=alpha),
        grid=(rows // tm,),
        in_specs=[_row_spec(tm, d), _full_spec(wo.shape), _row_spec(tm, d),
                  path.mod_spec(gt, tm, 1), _full_spec(lng.shape), _full_spec(lnb.shape),
                  path.mod_spec(sc, tm, 1), path.mod_spec(sh, tm, 1)],
        out_specs=[_row_spec(tm, d), _row_spec(tm, d)],
        out_shape=[jax.ShapeDtypeStruct((rows, d), _F32), jax.ShapeDtypeStruct((rows, d), _BF)],
        compiler_params=_params("parallel"),
        name="out_proj_ln",
    )(mg, wo, x, gt, lng, lnb, sc, sh)


def _ffn(h2, x1, prev, wa, wb, cw, cb, wd, gt, lng, lnb, sc, sh, path, alpha):
    rows, d = x1.shape
    dff = wa.shape[1]
    tm = path.tm(256 if path.dense else 512)
    tf = _tile(dff, 512, LANE)
    nm, nf = rows // tm, dff // tf
    gt, sc, sh = path.mod(gt, tm), path.mod(sc, tm), path.mod(sh, tm)
    row2 = pl.BlockSpec((tm, d), lambda i, j: (i, 0))
    common_in = [row2,
                 pl.BlockSpec((d, tf), lambda i, j: (0, j)),
                 pl.BlockSpec((d, tf), lambda i, j: (0, j)),
                 pl.BlockSpec((CONV_W, tf), lambda i, j: (0, j)),
                 pl.BlockSpec((1, tf), lambda i, j: (0, j)),
                 pl.BlockSpec((tf, d), lambda i, j: (j, 0)),
                 path.mod_spec(gt, tm, 2), pl.BlockSpec((1, d), lambda i, j: (0, 0)),
                 pl.BlockSpec((1, d), lambda i, j: (0, 0)),
                 path.mod_spec(sc, tm, 2), path.mod_spec(sh, tm, 2)]
    common_args = [x1, wa, wb, cw, cb, wd, gt, lng, lnb, sc, sh]
    if path.dense:
        seg = path.t
        rs = jnp.arange(rows) % seg
        sid = jnp.arange(rows) // seg
        p1 = jnp.where((rs == 0)[:, None], prev[sid, 1], 0.0)
        p2 = jnp.where((rs < 2)[:, None], prev[sid, jnp.minimum(rs, 1)], 0.0)
        tail = tm
        kern = functools.partial(_ffn_seg_kernel, seg=seg, alpha=alpha)
        in_specs = [row2, pl.BlockSpec((tm, tf), lambda i, j: (i, j)),
                    pl.BlockSpec((tm, tf), lambda i, j: (i, j))] + common_in
        args = [h2, p1, p2] + common_args
    else:
        hr = 16
        tail = 8
        kern = functools.partial(_ffn_halo_kernel, tiles_per_seq=path.t // tm, alpha=alpha)
        in_specs = [row2, pl.BlockSpec((hr, d), lambda i, j: (jnp.maximum(i * (tm // hr) - 1, 0), 0))
                    ] + common_in
        args = [h2, h2] + common_args
    y, hn, at = pl.pallas_call(
        kern,
        grid=(nm, nf),
        in_specs=in_specs,
        out_specs=[row2, row2, pl.BlockSpec((None, tail, tf), lambda i, j: (i, 0, j))],
        out_shape=[jax.ShapeDtypeStruct((rows, d), _F32), jax.ShapeDtypeStruct((rows, d), _BF),
                   jax.ShapeDtypeStruct((nm, tail, dff), _F32)],
        scratch_shapes=[pltpu.VMEM((tm, d), _F32)] + ([] if path.dense else (
            [pltpu.VMEM((tm + FFN_HALO, tf // FFN_GROUPS), _F32)] * FFN_GROUPS
            + [pltpu.VMEM((tm, tf // FFN_GROUPS), _F32)] * FFN_GROUPS
            + [pltpu.VMEM((tm, tf // FFN_GROUPS), _BF)] * FFN_GROUPS)),
        compiler_params=_params("parallel", "arbitrary"),
        name="conv_ffn_ln",
    )(*args)
    if path.dense:
        conv_new = at.reshape(path.nb, path.t, dff)[:, path.t - (CONV_W - 1):]
    else:
        tps = path.t // tm
        conv_new = at.reshape(path.nb, tps, tail, dff)[:, tps - 1, tail - (CONV_W - 1):]
    return y, hn, conv_new


def _layer(x, h, path, mods, nxt, cache, w, alpha):
    rows, d = x.shape
    nb, t = path.nb, path.t
    sh1, sc1, gt1, sh2, sc2, gt2 = mods
    cos, sin = path.rope_tables()
    tm_big = path.tm(1024)

    q_cat, ckv, kr128 = _latent(h, w["w_lat"], w["g_q"], w["g_kv"], w["w_uq"], cos, sin, path)
    if cache is None:
        k_t, v_a = _kvup_t(ckv, kr128, w["w_uk"].T, w["w_uv"], _mla_tiles(t)[1])
        oa = _mla_prompt(q_cat, k_t, v_a, nb, t)
    else:
        k_cat, v_a = _kvup(ckv, kr128, w["w_uk"], w["w_uv"])
        past = cache["ckv"].shape[2]
        kc, vc = _kvup(cache["ckv"].reshape(-1, KV_LORA), cache["krope"], w["w_uk"], w["w_uv"],
                       rows=nb * past, row_off=cache["layer"] * nb * past)
        q_pos = (path.pos0 + np.arange(t))[:, None] // CHUNK
        ok1 = (np.arange(past)[None, :] // CHUNK) <= q_pos
        ok2 = ((past + np.arange(t))[None, :] // CHUNK) <= q_pos
        t1 = jnp.asarray(np.where(ok1, 0.0, NEG)[None], _F32)
        t2 = jnp.asarray(np.where(ok2, 0.0, NEG)[None], _F32)
        oa = _attn2(q_cat.reshape(nb, t, -1), kc.reshape(nb, past, -1), vc.reshape(nb, past, -1),
                    k_cat.reshape(nb, t, -1), v_a.reshape(nb, t, -1), t1, t2,
                    QK_CAT, DVA, V_CAT, None, True, 0, 0, 0, "mla_attention_cached")
        oa = oa.reshape(rows, HA * DVA)

    wb3 = HB * DHB
    qkv = _mm(h, w["w_band"], _BF, tm_big, _tile(3 * wb3, 1024, LANE), "band_qkv")
    if cache is None:
        ob = _band_prompt(qkv, w["rel_bias"], nb, t)
        keep = min(BAND_PREV * CHUNK, t)
        tk = _tile(keep, 512)
        per = keep // tk
        kv_new = _mm(h, w["w_band"][:, wb3:], _F32, tk, _tile(2 * wb3, 1024, LANE), "band_kv_tail",
                     row_map=lambda i: (i // per) * (t // tk) + (t - keep) // tk + i % per,
                     out_rows=nb * keep)
        bk_new = kv_new[:, :wb3].reshape(nb, keep, HB, DHB)
        bv_new = kv_new[:, wb3:].reshape(nb, keep, HB, DHB)
    else:
        kv_new = _mm(h, w["w_band"][:, wb3:], _F32, tm_big, _tile(2 * wb3, 1024, LANE), "band_kv_tail")
        bk_new = kv_new[:, :wb3].reshape(nb, t, HB, DHB)
        bv_new = kv_new[:, wb3:].reshape(nb, t, HB, DHB)
        nk = cache["band_k"].shape[2]
        t1 = _rel_table(w["rel_bias"], nk, t, nk, (path.pos0 - nk + np.arange(nk) >= 0)[None, :])
        t2 = _rel_table(w["rel_bias"], 0, t, t, np.ones((t, t), bool))
        qkv3 = qkv.reshape(nb, t, 3 * wb3)
        ob = _attn2(qkv3, cache["band_k"].reshape(-1, nb, nk, wb3),
                    cache["band_v"].reshape(-1, nb, nk, wb3),
                    qkv3, qkv3, t1, t2, DHB, DHB, DHB, DHB ** -0.5, False, 0, 1, 2,
                    "band_attention_cached", layer=cache["layer"])
        ob = ob.reshape(rows, wb3)

    tm_r = path.tm(1024)
    qk_r = pl.pallas_call(
        _retqk_kernel,
        grid=(rows // tm_r,),
        in_specs=[_row_spec(tm_r, d), _full_spec(w["w_retqk"].shape),
                  path.rope_spec(cos, tm_r, 1), path.rope_spec(sin, tm_r, 1)],
        out_specs=_row_spec(tm_r, 2 * HR * DKR),
        out_shape=jax.ShapeDtypeStruct((rows, 2 * HR * DKR), _F32),
        compiler_params=_params("parallel"),
        name="ret_qk_rope",
    )(h, w["w_retqk"], cos, sin)
    v_r = _mm(h, w["w_retv"], _BF, tm_big, HR * DVR, "ret_v")
    g_r = _mm(h, w["w_retg"], _F32, tm_big, HR * DVR, "ret_gate")
    s0 = jnp.zeros((nb, HR, DKR, DVR), _F32) if cache is None else cache["ret"]
    oc, s_new = _retention(qk_r, v_r, g_r, s0, w["g_rn"], nb, t, 256)

    mg = _merge(h, oa, ob, oc, w["w_gates"], w["w_pa"], w["w_pb"], w["w_pc"], path)
    x1, h2 = _out_proj(mg, w["w_o"], x, gt1, w["ln1_g"], w["ln1_b"], sc2, sh2, path, alpha)

    prev = None if cache is None else cache["conv"]
    x2, hn, conv_new = _ffn(h2, x1, prev, w["w_fa"], w["w_fb"], w["cw"], w["cb"], w["w_fd"],
                            gt2, w["ln2_g"], w["ln2_b"], nxt[0], nxt[1], path, alpha)
    state = (ckv.reshape(nb, t, KV_LORA), kr128[:, :DR].reshape(nb, t, DR), bk_new, bv_new,
             s_new, conv_new)
    return x2, hn, state


def _layer_weights(l, wi, w_uq, w_ukv, w_ada_unused, p):
    d = wi.shape[1]
    o = np.cumsum([0, Q_LORA, KV_LORA, DR, HB * DHB, HB * DHB, HB * DHB, HR * DKR, HR * DKR,
                   HR * DVR, HR * DVR, d, d, d]).tolist()
    wl = wi[l]
    lat = jnp.concatenate([wl[:, :o[3]], jnp.zeros((d, LANE - DR), _BF)], axis=1)
    uq = w_uq[l].reshape(Q_LORA, HA, DN + DR)
    uq = jnp.pad(uq, ((0, 0), (0, 0), (0, QK_CAT - DN - DR))).reshape(Q_LORA, HA * QK_CAT)
    ukv = w_ukv[l].reshape(KV_LORA, HA, DN + DVA)
    row = lambda a: a[l][None, :]
    return {
        "w_lat": lat, "w_uq": uq,
        "w_uk": ukv[:, :, :DN].reshape(KV_LORA, HA * DN),
        "w_uv": ukv[:, :, DN:].reshape(KV_LORA, HA * DVA),
        "w_band": wl[:, o[3]:o[6]], "w_retqk": wl[:, o[6]:o[8]], "w_retv": wl[:, o[8]:o[9]],
        "w_retg": wl[:, o[9]:o[10]], "w_gates": wl[:, o[10]:o[13]],
        "g_q": row(p["g_q_lora"]), "g_kv": row(p["g_kv_lora"]), "rel_bias": p["rel_bias"][l],
        "g_rn": row(p["g_ret_norm"]),
        "w_pa": p["w_branch_a"][l], "w_pb": p["w_branch_b"][l], "w_pc": p["w_branch_c"][l],
        "w_o": p["w_o"][l], "ln1_g": row(p["ln1_g"]), "ln1_b": row(p["ln1_b"]),
        "w_fa": p["w_ff_a"][l], "w_fb": p["w_ff_b"][l], "cw": p["conv_w"][l],
        "cb": row(p["conv_b"]), "w_fd": p["w_ff_down"][l],
        "ln2_g": row(p["ln2_g"]), "ln2_b": row(p["ln2_b"]),
    }


def kernel(x_prompt, x_sample, c_prompt, c_sample, cache_mla_ckv, cache_mla_krope, cache_band_k, cache_band_v, state_ret, state_conv, w_ada, b_ada, w_in, g_q_lora, g_kv_lora, w_uq, w_ukv, rel_bias, g_ret_norm, w_branch_a, w_branch_b, w_branch_c, w_o, ln1_g, ln1_b, w_ff_a, w_ff_b, conv_w, conv_b, w_ff_down, ln2_g, ln2_b):
    nb_p, t_p, d = x_prompt.shape
    nb_s, t_s, _ = x_sample.shape
    depth = w_in.shape[0]
    past = cache_mla_ckv.shape[2]
    alpha = (2 * depth) ** 0.25

    nc = nb_p + nb_s
    ncp = -(-nc // 16) * 16
    c_all = jnp.pad(jnp.concatenate([c_prompt, c_sample], 0), ((0, ncp - nc), (0, 0)))
    tn_a = _tile(6 * d, 2048, LANE)
    ada = pl.pallas_call(
        _ada_kernel,
        grid=(depth, 6 * d // tn_a),
        in_specs=[pl.BlockSpec((ncp, d), lambda l, j: (0, 0)),
                  pl.BlockSpec((None, d, tn_a), lambda l, j: (l, 0, j)),
                  pl.BlockSpec((None, 1, tn_a), lambda l, j: (l, 0, j))],
        out_specs=pl.BlockSpec((None, ncp, tn_a), lambda l, j: (l, 0, j)),
        out_shape=jax.ShapeDtypeStruct((depth, ncp, 6 * d), _F32),
        compiler_params=_params("parallel", "arbitrary"),
        name="ada_ln",
    )(c_all, w_ada.astype(_BF), b_ada[:, None, :])

    def mods(l, lo, n):
        a = ada[l, lo:lo + n]
        sh1, sc1, gt1, sh2, sc2, gt2 = [a[:, k * d:(k + 1) * d] for k in range(6)]
        return (sh1, 1.0 + sc1, 1.0 + gt1, sh2, 1.0 + sc2, 1.0 + gt2)

    bf = lambda a: a.astype(_BF)
    p = dict(g_q_lora=g_q_lora, g_kv_lora=g_kv_lora, rel_bias=rel_bias, g_ret_norm=g_ret_norm,
             w_branch_a=bf(w_branch_a), w_branch_b=bf(w_branch_b), w_branch_c=bf(w_branch_c),
             w_o=bf(w_o), ln1_g=ln1_g, ln1_b=ln1_b, w_ff_a=bf(w_ff_a), w_ff_b=bf(w_ff_b),
             conv_w=conv_w, conv_b=conv_b, w_ff_down=bf(w_ff_down), ln2_g=ln2_g, ln2_b=ln2_b)
    wi, wq, wkv = bf(w_in), bf(w_uq), bf(w_ukv)
    krope_cache = jnp.pad(cache_mla_krope.reshape(-1, DR), ((0, 0), (0, LANE - DR)))

    streams = [(_Path(nb_p, t_p, 0, False), x_prompt, 0),
               (_Path(nb_s, t_s, past, True), x_sample, nb_p)]
    outs = []
    for path, x0, lo in streams:
        x = x0.reshape(path.rows, d)
        m0 = mods(0, lo, path.nb)
        h = _modulate(x, m0[1], m0[0], path)
        states = []
        for l in range(depth):
            m = mods(l, lo, path.nb)
            mn = mods(min(l + 1, depth - 1), lo, path.nb)
            cache = None
            if path.dense:
                cache = dict(layer=l, ckv=cache_mla_ckv, krope=krope_cache, band_k=cache_band_k,
                             band_v=cache_band_v, ret=state_ret[l], conv=state_conv[l])
            w = _layer_weights(l, wi, wq, wkv, None, p)
            x, h, st = _layer(x, h, path, m, (mn[1], mn[0]), cache, w, alpha)
            states.append(st)
        outs.append((x.reshape(path.nb, path.t, d), [jnp.stack(z) for z in zip(*states)]))
    (y_p, st_p), (y_s, st_s) = outs
    return (y_p, y_s, *st_p, *st_s)
```

```python
import functools

import numpy as np
import jax
import jax.numpy as jnp
from jax import lax
from jax.experimental import pallas as pl
from jax.experimental.pallas import tpu as pltpu

CHUNK = 64
HA, DN, DR, DVA = 8, 128, 64, 128
Q_LORA, KV_LORA = 512, 256
HB, DHB = 8, 128
BAND_PREV, REL_CLIP = 8, 128
HR, DKR, DVR = 8, 64, 128
CONV_W = 3
ROPE_THETA = 10000.0
EPS = 1e-5
LANE = 128
QK_CAT = DN + 2 * DR
NEG = -1e30
VMEM_LIMIT = 56 * 1024 * 1024
LOG2E = float(np.log2(np.e))
MLA_QSCALE = (DN + DR) ** -0.5 * LOG2E
V_CAT = 2 * DVA
FFN_HALO = 8
FFN_ROWS = 64
FFN_GROUPS = 2

_BF = jnp.bfloat16
_F32 = jnp.float32


def _dot(a, b):
    return jnp.dot(a, b, preferred_element_type=_F32)


def _dot_nt(a, b):
    return lax.dot_general(a, b, (((1,), (1,)), ((), ())), preferred_element_type=_F32)


def _dot_tn(a, b):
    return lax.dot_general(a, b, (((0,), (0,)), ((), ())), preferred_element_type=_F32)


def _tile(n, pref, step=8):
    if n <= pref:
        return n
    t = pref - pref % step
    while n % t:
        t -= step
    return t


def _params(*sem):
    return pltpu.CompilerParams(dimension_semantics=sem, vmem_limit_bytes=VMEM_LIMIT)


def _swap32(x):
    n = x.shape[-1]
    lane = lax.broadcasted_iota(jnp.int32, x.shape, 1)
    fwd = pltpu.roll(x, n - DR // 2, 1)
    bwd = pltpu.roll(x, DR // 2, 1)
    return jnp.where((lane % DR) < DR // 2, fwd, bwd)


def _rope128(x, cos, sin):
    return x * cos + _swap32(x) * sin


def _layer_norm(z, g, b):
    mu = jnp.mean(z, -1, keepdims=True)
    zc = z - mu
    var = jnp.mean(zc * zc, -1, keepdims=True)
    return zc * lax.rsqrt(var + EPS) * g + b


def _rms(x, g):
    return x * lax.rsqrt(jnp.mean(x * x, -1, keepdims=True) + EPS) * g


def _ada_kernel(c_ref, w_ref, b_ref, o_ref):
    c = c_ref[...]
    s = (c * jax.nn.sigmoid(c)).astype(_BF)
    o_ref[...] = _dot(s, w_ref[...]) + b_ref[...]


def _mod_kernel(x_ref, sc_ref, sh_ref, o_ref):
    o_ref[...] = (x_ref[...] * sc_ref[...] + sh_ref[...]).astype(o_ref.dtype)


def _mm_kernel(x_ref, w_ref, o_ref, *, first_block_scale):
    y = _dot(x_ref[...], w_ref[...])
    if first_block_scale is not None:
        y = y * jnp.where(pl.program_id(1) == 0, first_block_scale, 1.0)
    o_ref[...] = y.astype(o_ref.dtype)


def _retqk_kernel(x_ref, w_ref, cos_ref, sin_ref, o_ref):
    y = _dot(x_ref[...], w_ref[...])
    cos, sin = cos_ref[...], sin_ref[...]
    nq = HR * DKR // LANE
    for c in range(2 * nq):
        blk = _rope128(y[:, c * LANE:(c + 1) * LANE], cos, sin)
        if c < nq:
            blk = blk * (DKR ** -0.5)
        o_ref[:, c * LANE:(c + 1) * LANE] = blk


def _lat_kernel(x_ref, wl_ref, gq_ref, gkv_ref, wuq_ref, cos_ref, sin_ref,
                q_ref, ckv_ref, kr_ref):
    y = _dot(x_ref[...], wl_ref[...])
    cos, sin = cos_ref[...], sin_ref[...]
    cqn = _rms(y[:, :Q_LORA], gq_ref[...])
    ckv_ref[...] = _rms(y[:, Q_LORA:Q_LORA + KV_LORA], gkv_ref[...])
    kr_ref[...] = _rope128(y[:, Q_LORA + KV_LORA:], cos, sin)
    qa = _dot(cqn.astype(_BF), wuq_ref[...]) * MLA_QSCALE
    for h in range(HA):
        o = h * QK_CAT
        q_ref[:, o:o + DN] = qa[:, o:o + DN].astype(_BF)
        q_ref[:, o + DN:o + QK_CAT] = _rope128(qa[:, o + DN:o + QK_CAT], cos, sin).astype(_BF)


def _kvup_kernel(ckv_ref, kr_ref, wk_ref, wv_ref, k_ref, v_ref):
    c = ckv_ref[...].astype(_BF)
    kn = _dot(c, wk_ref[...])
    vn = _dot(c, wv_ref[...])
    krb = kr_ref[...].astype(_BF)
    ones = jnp.ones((c.shape[0], DVA), _BF)
    for h in range(HA):
        o = h * QK_CAT
        k_ref[:, o:o + DN] = kn[:, h * DN:(h + 1) * DN].astype(_BF)
        k_ref[:, o + DN:o + QK_CAT] = krb
        v_ref[:, h * V_CAT:h * V_CAT + DVA] = vn[:, h * DVA:(h + 1) * DVA].astype(_BF)
        v_ref[:, h * V_CAT + DVA:(h + 1) * V_CAT] = ones


def _kvup_t_kernel(ckv_ref, kr_ref, wkt_ref, wv_ref, kt_ref, v_ref):
    c = ckv_ref[...].astype(_BF)
    knt = _dot_nt(wkt_ref[...], c)
    vn = _dot(c, wv_ref[...])
    krt = kr_ref[...].T.astype(_BF)
    ones = jnp.ones((c.shape[0], DVA), _BF)
    for h in range(HA):
        o = h * QK_CAT
        kt_ref[o:o + DN, :] = knt[h * DN:(h + 1) * DN, :].astype(_BF)
        kt_ref[o + DN:o + QK_CAT, :] = krt
        v_ref[:, h * V_CAT:h * V_CAT + DVA] = vn[:, h * DVA:(h + 1) * DVA].astype(_BF)
        v_ref[:, h * V_CAT + DVA:(h + 1) * V_CAT] = ones


def _mla_kernel(q_ref, kt_ref, v_ref, o_ref, s_scr, p_scr, m_scr, a_scr, acc_scr, *, hg, tq, tk):
    qi = pl.program_id(2)
    kc = tk // CHUNK
    m_scr[...] = jnp.full(m_scr.shape, NEG, _F32)
    acc_scr[...] = jnp.zeros(acc_scr.shape, _F32)
    col = lax.broadcasted_iota(jnp.int32, (CHUNK, tk), 1)

    def scores(kt, slot):
        for h in range(hg):
            s_scr[slot, h] = _dot(q_ref[:, h * QK_CAT:(h + 1) * QK_CAT],
                                  kt_ref[kt, h * QK_CAT:(h + 1) * QK_CAT, :])

    def consume(kt, slot, diag):
        start = pl.multiple_of(kt * tk, tk)
        for h in range(hg):
            for r in range(tq // CHUNK):
                rows = slice(r * CHUNK, (r + 1) * CHUNK)
                ncol = tk if diag is None else min(max((r - diag * kc + 1) * CHUNK, 0), tk)
                if ncol == 0:
                    p_scr[h, rows, :] = jnp.zeros((CHUNK, tk), _BF)
                    a_scr[h, rows, :] = jnp.ones((CHUNK, 1), _F32)
                    continue
                s = s_scr[slot, h, rows, :]
                if ncol < tk:
                    s = jnp.where(col < ncol, s, NEG)
                m_old = m_scr[h, rows, :]
                m_new = jnp.maximum(m_old, jnp.max(s, -1, keepdims=True))
                p_scr[h, rows, :] = jnp.exp2(s - m_new).astype(_BF)
                a_scr[h, rows, :] = jnp.exp2(m_old - m_new)
                m_scr[h, rows, :] = m_new
            acc_scr[h] = a_scr[h] * acc_scr[h] + _dot(
                p_scr[h], v_ref[pl.ds(start, tk), h * V_CAT:(h + 1) * V_CAT])

    def pair(j, c):
        scores(2 * j + 1, 1)
        consume(2 * j, 0, None)
        scores(2 * j + 2, 0)
        consume(2 * j + 1, 1, None)
        return c

    scores(0, 0)
    lax.fori_loop(0, qi, pair, 0)
    scores(2 * qi + 1, 1)
    consume(2 * qi, 0, 0)
    consume(2 * qi + 1, 1, 1)
    for h in range(hg):
        acc = acc_scr[h]
        o_ref[:, h * DVA:(h + 1) * DVA] = (acc[:, :DVA] / acc[:, DVA:]).astype(o_ref.dtype)


def _band_kernel(q_ref, k_ref, v_ref, tb_ref, o_ref, *, hg, win, chunks_per_tile):
    n = pl.program_id(2)
    ws = pl.multiple_of(jnp.maximum(n * chunks_per_tile - BAND_PREV, 0) * CHUNK, CHUNK)
    for h in range(hg):
        q = q_ref[:, h * DHB:(h + 1) * DHB]
        k = k_ref[pl.ds(ws, win), h * DHB:(h + 1) * DHB]
        v = v_ref[pl.ds(ws, win), h * DHB:(h + 1) * DHB]
        s = _dot_nt(q, k) + tb_ref[h]
        p = jnp.exp2(s - jnp.max(s, -1, keepdims=True))
        l = jnp.sum(p, -1, keepdims=True)
        o_ref[:, h * DHB:(h + 1) * DHB] = (_dot(p.astype(_BF), v) / l).astype(o_ref.dtype)


def _attn2_kernel(q_ref, k1_ref, v1_ref, k2_ref, v2_ref, t1_ref, t2_ref, o_ref,
                  *, heads, dq, dv, v_stride, head_axis, per_head_table):
    for h in range(heads):
        th = h if per_head_table else 0
        q = q_ref[:, h * dq:(h + 1) * dq]
        if head_axis:
            k1, v1 = k1_ref[:, h, :].astype(_BF), v1_ref[:, h, :].astype(_BF)
        else:
            k1 = k1_ref[:, h * dq:(h + 1) * dq].astype(_BF)
            v1 = v1_ref[:, h * v_stride:h * v_stride + dv].astype(_BF)
        k2 = k2_ref[:, h * dq:(h + 1) * dq].astype(_BF)
        v2 = v2_ref[:, h * v_stride:h * v_stride + dv].astype(_BF)
        s1 = _dot_nt(q, k1) + t1_ref[th]
        s2 = _dot_nt(q, k2) + t2_ref[th]
        m = jnp.maximum(jnp.max(s1, -1, keepdims=True), jnp.max(s2, -1, keepdims=True))
        p1 = jnp.exp2(s1 - m)
        p2 = jnp.exp2(s2 - m)
        l = jnp.sum(p1, -1, keepdims=True) + jnp.sum(p2, -1, keepdims=True)
        o = _dot(p1.astype(_BF), v1) + _dot(p2.astype(_BF), v2)
        o_ref[:, h * dv:(h + 1) * dv] = (o / l).astype(o_ref.dtype)


def _ret_kernel(qk_ref, v_ref, gr_ref, s0_ref, dq_ref, dk_ref, dm_ref, ds_ref, grn_ref,
                o_ref, sout_ref, s_scr):
    ci = pl.program_id(1)
    npair = HR // 2
    r2 = lax.broadcasted_iota(jnp.int32, (2 * DKR, 2 * DVR), 0) // DKR
    c2 = lax.broadcasted_iota(jnp.int32, (2 * DKR, 2 * DVR), 1) // DVR
    on_diag = r2 == c2

    @pl.when(ci == 0)
    def _():
        for p in range(npair):
            top = jnp.concatenate([s0_ref[2 * p], jnp.zeros((DKR, DVR), _F32)], axis=1)
            bot = jnp.concatenate([jnp.zeros((DKR, DVR), _F32), s0_ref[2 * p + 1]], axis=1)
            s_scr[p] = jnp.concatenate([top, bot], axis=0)

    lane = lax.broadcasted_iota(jnp.int32, (qk_ref.shape[0], LANE), 1)
    for p in range(npair):
        q2 = qk_ref[:, p * LANE:(p + 1) * LANE]
        k2 = qk_ref[:, HR * DKR + p * LANE:HR * DKR + (p + 1) * LANE]
        v2 = v_ref[:, 2 * p * DVR:(2 * p + 2) * DVR]
        k2b = k2.astype(_BF)
        s2 = s_scr[p]
        cross = _dot((q2 * dq_ref[p]).astype(_BF), s2.astype(_BF))
        for e in range(2):
            h = 2 * p + e
            qm = jnp.where((lane // DKR) == e, q2, 0.0).astype(_BF)
            inner = _dot_nt(qm, k2b) * dm_ref[h]
            o = _dot(inner.astype(_BF), v2[:, e * DVR:(e + 1) * DVR])
            o = o + cross[:, e * DVR:(e + 1) * DVR]
            mu = jnp.mean(o, -1, keepdims=True)
            oc = o - mu
            var = jnp.mean(oc * oc, -1, keepdims=True)
            on = oc * lax.rsqrt(var + EPS) * grn_ref[:, h * DVR:(h + 1) * DVR]
            g = gr_ref[:, h * DVR:(h + 1) * DVR]
            o_ref[:, h * DVR:(h + 1) * DVR] = (g * jax.nn.sigmoid(g) * on).astype(o_ref.dtype)
        upd = _dot_tn((k2 * dk_ref[p]).astype(_BF), v2)
        s_scr[p] = s2 * ds_ref[p] + jnp.where(on_diag, upd, 0.0)

    @pl.when(ci == pl.num_programs(1) - 1)
    def _():
        for p in range(npair):
            sp = s_scr[p]
            sout_ref[2 * p] = sp[:DKR, :DVR]
            sout_ref[2 * p + 1] = sp[DKR:, DVR:]


def _merge_kernel(h_ref, oa_ref, ob_ref, oc_ref, wga_ref, wgb_ref, wgc_ref,
                  wpa_ref, wpb_ref, wpc_ref, o_ref):
    h = h_ref[...]

    def branch(o_r, wg_r, wp_r):
        return jax.nn.sigmoid(_dot(h, wg_r[...])) * _dot(o_r[...], wp_r[...])

    m = branch(oa_ref, wga_ref, wpa_ref) + branch(ob_ref, wgb_ref, wpb_ref)
    m = m + branch(oc_ref, wgc_ref, wpc_ref)
    o_ref[...] = m.astype(o_ref.dtype)


def _out_kernel(mg_ref, wo_ref, x_ref, gt_ref, lng_ref, lnb_ref, sc_ref, sh_ref,
                x1_ref, h2_ref, *, alpha):
    y = _dot(mg_ref[...], wo_ref[...])
    x1 = _layer_norm(alpha * x_ref[...] + gt_ref[...] * y, lng_ref[...], lnb_ref[...])
    x1_ref[...] = x1
    h2_ref[...] = (x1 * sc_ref[...] + sh_ref[...]).astype(h2_ref.dtype)


def _gated(a, a1, a2, bb, cw, cb):
    conv = cb + a2 * cw[0:1, :] + a1 * cw[1:2, :] + a * cw[2:3, :]
    gelu = 0.5 * conv * (1.0 + lax.erf(conv * (2.0 ** -0.5)))
    return (gelu * bb).astype(_BF)


def _ffn_clear(acc_ref):
    @pl.when(pl.program_id(1) == 0)
    def _():
        acc_ref[...] = jnp.zeros(acc_ref.shape, _F32)


def _ffn_norm(x_ref, gt_ref, lng_ref, lnb_ref, sc_ref, sh_ref, y_ref, hn_ref, acc_ref, alpha):
    @pl.when(pl.program_id(1) == pl.num_programs(1) - 1)
    def _():
        z = alpha * x_ref[...] + gt_ref[...] * acc_ref[...]
        y = _layer_norm(z, lng_ref[...], lnb_ref[...])
        y_ref[...] = y
        hn_ref[...] = (y * sc_ref[...] + sh_ref[...]).astype(hn_ref.dtype)


def _ffn_halo_kernel(h_ref, halo_ref, x_ref, wa_ref, wb_ref, cw_ref, cb_ref, wd_ref, gt_ref,
                     lng_ref, lnb_ref, sc_ref, sh_ref, y_ref, hn_ref, at_ref, acc_ref,
                     *scr, tiles_per_seq, alpha):
    ng = len(scr) // 3
    a_scr, b_scr, g_scr = scr[:ng], scr[ng:2 * ng], scr[2 * ng:]
    i = pl.program_id(0)
    h = h_ref[...]
    halo = halo_ref[...]
    tm = h.shape[0]
    w = wa_ref.shape[1] // ng
    keep = jnp.where(i % tiles_per_seq == 0, 0.0, 1.0)
    _ffn_clear(acc_ref)
    for c in range(ng):
        cs = slice(c * w, (c + 1) * w)
        a_scr[c][FFN_HALO:, :] = _dot(h, wa_ref[:, cs])
        b_scr[c][...] = _dot(h, wb_ref[:, cs])
        ah = _dot(halo, wa_ref[:, cs])
        a_scr[c][:FFN_HALO, :] = ah[ah.shape[0] - FFN_HALO:, :] * keep
        at_ref[:, cs] = a_scr[c][tm:, :]
    rc = min(FFN_ROWS, tm)
    for c in range(ng):
        cs = slice(c * w, (c + 1) * w)
        cw, cb = cw_ref[:, cs], cb_ref[:, cs]
        for k in range(tm // rc):
            r = FFN_HALO + k * rc
            g_scr[c][r - FFN_HALO:r - FFN_HALO + rc, :] = _gated(
                a_scr[c][r:r + rc, :], a_scr[c][r - 1:r - 1 + rc, :], a_scr[c][r - 2:r - 2 + rc, :],
                b_scr[c][r - FFN_HALO:r - FFN_HALO + rc, :], cw, cb)
        acc_ref[...] += _dot(g_scr[c][...], wd_ref[cs, :])
    _ffn_norm(x_ref, gt_ref, lng_ref, lnb_ref, sc_ref, sh_ref, y_ref, hn_ref, acc_ref, alpha)


def _ffn_seg_kernel(h_ref, p1_ref, p2_ref, x_ref, wa_ref, wb_ref, cw_ref, cb_ref, wd_ref,
                    gt_ref, lng_ref, lnb_ref, sc_ref, sh_ref, y_ref, hn_ref, at_ref, acc_ref,
                    *, seg, alpha):
    h = h_ref[...]
    a = _dot(h, wa_ref[...])
    bb = _dot(h, wb_ref[...])
    rs = lax.broadcasted_iota(jnp.int32, a.shape, 0) % seg
    a1 = jnp.where(rs == 0, p1_ref[...], pltpu.roll(a, 1, 0))
    a2 = jnp.where(rs < 2, p2_ref[...], pltpu.roll(a, 2, 0))
    at_ref[...] = a
    _ffn_clear(acc_ref)
    acc_ref[...] += _dot(_gated(a, a1, a2, bb, cw_ref[...], cb_ref[...]), wd_ref[...])
    _ffn_norm(x_ref, gt_ref, lng_ref, lnb_ref, sc_ref, sh_ref, y_ref, hn_ref, acc_ref, alpha)


class _Path:
    def __init__(self, nb, t, pos0, dense):
        self.nb, self.t, self.rows, self.pos0, self.dense = nb, t, nb * t, pos0, dense

    def tm(self, pref):
        return _tile(self.rows if self.dense else self.t, pref)

    def mod(self, v, tm):
        d = v.shape[-1]
        if self.dense:
            return jnp.repeat(v, self.t, axis=0).reshape(self.rows // tm, tm, d)
        return v[:, None, :]

    def mod_spec(self, arr, tm, rank):
        g, r, d = arr.shape
        tpg = (self.rows // tm) // g
        if rank == 1:
            return pl.BlockSpec((None, r, d), lambda i: (i // tpg, 0, 0))
        return pl.BlockSpec((None, r, d), lambda i, j: (i // tpg, 0, 0))

    def rope_tables(self):
        half = DR // 2
        inv = ROPE_THETA ** (-jnp.arange(half, dtype=_F32) / half)
        pos = self.pos0 + jnp.arange(self.t, dtype=jnp.int32)
        ang = pos.astype(_F32)[:, None] * inv[None, :]
        cos, sin = jnp.cos(ang), jnp.sin(ang)
        cos = jnp.concatenate([cos, cos, cos, cos], -1)
        sin = jnp.concatenate([-sin, sin, -sin, sin], -1)
        if self.dense:
            cos, sin = jnp.tile(cos, (self.nb, 1)), jnp.tile(sin, (self.nb, 1))
        return cos, sin

    def rope_spec(self, tab, tm, rank):
        nt = tab.shape[0] // tm
        if rank == 1:
            return pl.BlockSpec((tm, LANE), lambda i: (i % nt, 0))
        return pl.BlockSpec((tm, LANE), lambda i, j: (i % nt, 0))


def _row_spec(tm, n):
    return pl.BlockSpec((tm, n), lambda i: (i, 0))


def _full_spec(shape):
    nd = len(shape)
    return pl.BlockSpec(shape, lambda *_: (0,) * nd)


def _mm(x, w, out_dtype, tm, tn, name, row_map=None, out_rows=None, first_block_scale=None):
    rows, k = x.shape
    n = w.shape[1]
    out_rows = rows if out_rows is None else out_rows
    row_map = (lambda i: i) if row_map is None else row_map
    return pl.pallas_call(
        functools.partial(_mm_kernel, first_block_scale=first_block_scale),
        grid=(out_rows // tm, n // tn),
        in_specs=[pl.BlockSpec((tm, k), lambda i, j: (row_map(i), 0)),
                  pl.BlockSpec((k, tn), lambda i, j: (0, j))],
        out_specs=pl.BlockSpec((tm, tn), lambda i, j: (i, j)),
        out_shape=jax.ShapeDtypeStruct((out_rows, n), out_dtype),
        compiler_params=_params("parallel", "arbitrary"),
        name=name,
    )(x, w)


def _modulate(x, sc, sh, path):
    rows, d = x.shape
    tm = path.tm(1024)
    sc, sh = path.mod(sc, tm), path.mod(sh, tm)
    return pl.pallas_call(
        _mod_kernel,
        grid=(rows // tm,),
        in_specs=[_row_spec(tm, d), path.mod_spec(sc, tm, 1), path.mod_spec(sh, tm, 1)],
        out_specs=_row_spec(tm, d),
        out_shape=jax.ShapeDtypeStruct((rows, d), _BF),
        compiler_params=_params("parallel"),
        name="modulate",
    )(x, sc, sh)


def _latent(h, wl, gq, gkv, wuq, cos, sin, path):
    rows, d = h.shape
    tm = path.tm(512)
    nl = wl.shape[1]
    return pl.pallas_call(
        _lat_kernel,
        grid=(rows // tm,),
        in_specs=[_row_spec(tm, d), _full_spec(wl.shape), _full_spec(gq.shape),
                  _full_spec(gkv.shape), _full_spec(wuq.shape),
                  path.rope_spec(cos, tm, 1), path.rope_spec(sin, tm, 1)],
        out_specs=[_row_spec(tm, HA * QK_CAT), _row_spec(tm, KV_LORA), _row_spec(tm, LANE)],
        out_shape=[jax.ShapeDtypeStruct((rows, HA * QK_CAT), _BF),
                   jax.ShapeDtypeStruct((rows, KV_LORA), _F32),
                   jax.ShapeDtypeStruct((rows, LANE), _F32)],
        compiler_params=_params("parallel"),
        name="mla_latent",
    )(h, wl, gq, gkv, wuq, cos, sin)


def _kvup(ckv, kr, wk, wv, rows=None, row_off=0):
    rows = ckv.shape[0] if rows is None else rows
    tm = _tile(rows, 1024)
    off = row_off // tm
    return pl.pallas_call(
        _kvup_kernel,
        grid=(rows // tm,),
        in_specs=[pl.BlockSpec((tm, KV_LORA), lambda i: (off + i, 0)),
                  pl.BlockSpec((tm, LANE), lambda i: (off + i, 0)), _full_spec(wk.shape),
                  _full_spec(wv.shape)],
        out_specs=[_row_spec(tm, HA * QK_CAT), _row_spec(tm, HA * V_CAT)],
        out_shape=[jax.ShapeDtypeStruct((rows, HA * QK_CAT), _BF),
                   jax.ShapeDtypeStruct((rows, HA * V_CAT), _BF)],
        compiler_params=_params("parallel"),
        name="mla_kv_up",
    )(ckv, kr, wk, wv)


def _kvup_t(ckv, kr, wkt, wv, tk):
    rows = ckv.shape[0]
    return pl.pallas_call(
        _kvup_t_kernel,
        grid=(rows // tk,),
        in_specs=[_row_spec(tk, KV_LORA), _row_spec(tk, LANE), _full_spec(wkt.shape),
                  _full_spec(wv.shape)],
        out_specs=[pl.BlockSpec((None, HA * QK_CAT, tk), lambda i: (i, 0, 0)),
                   _row_spec(tk, HA * V_CAT)],
        out_shape=[jax.ShapeDtypeStruct((rows // tk, HA * QK_CAT, tk), _BF),
                   jax.ShapeDtypeStruct((rows, HA * V_CAT), _BF)],
        compiler_params=_params("parallel"),
        name="mla_kv_up_t",
    )(ckv, kr, wkt, wv)


def _mla_tiles(t):
    tk = _tile(t // 2, 512, CHUNK)
    assert t % (2 * tk) == 0
    return 2 * tk, tk


def _mla_prompt(q, kt, v, nb, t):
    hg = 2
    tq, tk = _mla_tiles(t)
    nq, nk = t // tq, t // tk
    kern = functools.partial(_mla_kernel, hg=hg, tq=tq, tk=tk)
    return pl.pallas_call(
        kern,
        grid=(nb, HA // hg, nq),
        in_specs=[pl.BlockSpec((tq, hg * QK_CAT), lambda b, g, i: (b * nq + i, g)),
                  pl.BlockSpec((nk, hg * QK_CAT, tk), lambda b, g, i: (b, g, 0)),
                  pl.BlockSpec((t, hg * V_CAT), lambda b, g, i: (b, g))],
        out_specs=pl.BlockSpec((tq, hg * DVA), lambda b, g, i: (b * nq + i, g)),
        out_shape=jax.ShapeDtypeStruct((nb * t, HA * DVA), _BF),
        scratch_shapes=[pltpu.VMEM((2, hg, tq, tk), _F32), pltpu.VMEM((hg, tq, tk), _BF),
                        pltpu.VMEM((hg, tq, 1), _F32), pltpu.VMEM((hg, tq, 1), _F32),
                        pltpu.VMEM((hg, tq, V_CAT), _F32)],
        compiler_params=_params("parallel", "parallel", "arbitrary"),
        name="mla_attention",
    )(q, kt, v)


def _band_tables(rel_bias, tq, t):
    r = tq // CHUNK
    win = (r + BAND_PREV) * CHUNK
    nv = -(-BAND_PREV // r) + 1
    i = np.arange(tq)[:, None]
    m = np.arange(win)[None, :]
    tabs = []
    for var in range(nv):
        n0 = var * r
        ws = max(n0 - BAND_PREV, 0)
        cq = n0 + i // CHUNK
        ck = ws + m // CHUNK
        allowed = (ck <= cq) & (ck >= cq - BAND_PREV)
        tabs.append(_rel_table(rel_bias, (n0 - ws) * CHUNK, tq, win, allowed))
    return jnp.stack(tabs), win, nv


def _rel_table(rel_bias, off, nq, nk, allowed):
    period = nq + nk
    u = np.arange(period)
    u = np.where(u < nk, u, u - period)
    pick = np.zeros((2 * REL_CLIP + 1, period), np.float32)
    pick[np.clip(off - u, -REL_CLIP, REL_CLIP) + REL_CLIP, np.arange(period)] = 1.0
    f = jnp.dot(rel_bias, pick, precision=lax.Precision.HIGHEST) * LOG2E
    circ = jnp.tile(f, (1, nq))[:, :nq * (period - 1)].reshape(-1, nq, period - 1)[:, :, :nk]
    return jnp.where(jnp.asarray(np.broadcast_to(allowed, (nq, nk)))[None], circ, NEG).astype(_F32)


def _band_prompt(qkv, rel_bias, nb, t):
    hg = 4
    tq = _tile(t, 256, CHUNK)
    nq = t // tq
    tabs, win, nv = _band_tables(rel_bias, tq, t)
    ng = HB // hg
    kern = functools.partial(_band_kernel, hg=hg, win=win, chunks_per_tile=tq // CHUNK)
    return pl.pallas_call(
        kern,
        grid=(nb, ng, nq),
        in_specs=[pl.BlockSpec((tq, hg * DHB), lambda b, g, i: (b * nq + i, g)),
                  pl.BlockSpec((t, hg * DHB), lambda b, g, i: (b, ng + g)),
                  pl.BlockSpec((t, hg * DHB), lambda b, g, i: (b, 2 * ng + g)),
                  pl.BlockSpec((None, hg, tq, win),
                               lambda b, g, i: (jnp.minimum(i, nv - 1), g, 0, 0))],
        out_specs=pl.BlockSpec((tq, hg * DHB), lambda b, g, i: (b * nq + i, g)),
        out_shape=jax.ShapeDtypeStruct((nb * t, HB * DHB), _BF),
        compiler_params=_params("parallel", "parallel", "arbitrary"),
        name="band_attention",
    )(qkv, qkv, qkv, tabs)


def _attn2(q, k1, v1, k2, v2, t1, t2, dq, dv, v_stride, qoff, koff, voff, name, layer=None):
    nb, tq, _ = q.shape
    heads = HA
    n1 = k1.shape[1] if layer is None else k1.shape[2]
    kern = functools.partial(_attn2_kernel, heads=heads, dq=dq, dv=dv, v_stride=v_stride,
                             head_axis=layer is not None, per_head_table=t1.shape[0] > 1)

    def cached(width):
        if layer is None:
            return pl.BlockSpec((None, n1, width), lambda b: (b, 0, 0))
        return pl.BlockSpec((None, None, n1, heads, width // heads),
                            lambda b: (layer, b, 0, 0, 0))

    return pl.pallas_call(
        kern,
        grid=(nb,),
        in_specs=[pl.BlockSpec((None, tq, heads * dq), lambda b: (b, 0, qoff)),
                  cached(heads * dq), cached(heads * v_stride),
                  pl.BlockSpec((None, tq, heads * dq), lambda b: (b, 0, koff)),
                  pl.BlockSpec((None, tq, heads * v_stride), lambda b: (b, 0, voff)),
                  _full_spec(t1.shape), _full_spec(t2.shape)],
        out_specs=pl.BlockSpec((None, tq, heads * dv), lambda b: (b, 0, 0)),
        out_shape=jax.ShapeDtypeStruct((nb, tq, heads * dv), _BF),
        compiler_params=_params("parallel"),
        name=name,
    )(q, k1, v1, k2, v2, t1, t2)


def _ret_tables(lc):
    lg = np.log1p(-np.exp2(-5.0 - np.arange(HR, dtype=np.float64)))
    idx = np.arange(lc, dtype=np.float64)
    lane_head = np.arange(LANE) // DKR
    npair = HR // 2
    dq = np.stack([np.exp((idx[:, None] + 1.0) * lg[2 * p + lane_head][None, :])
                   for p in range(npair)])
    dk = np.stack([np.exp((lc - 1.0 - idx)[:, None] * lg[2 * p + lane_head][None, :])
                   for p in range(npair)])
    diff = idx[:, None] - idx[None, :]
    dm = np.stack([np.where(diff >= 0, np.exp(np.maximum(diff, 0.0) * lg[h]), 0.0)
                   for h in range(HR)])
    rh = np.arange(2 * DKR)[:, None] // DKR
    ch = np.arange(2 * DVR)[None, :] // DVR
    ds = np.stack([np.where(rh == ch, np.exp(lc * lg[2 * p + rh]) + 0.0 * ch, 0.0)
                   for p in range(npair)])
    return [jnp.asarray(a, _F32) for a in (dq, dk, dm, ds)]


def _retention(qk, v, gr, s0, grn, nb, t, lc_pref):
    lc = _tile(t, lc_pref, CHUNK) if t > CHUNK else t
    nc = t // lc
    dq, dk, dm, ds = _ret_tables(lc)
    w = HR * DVR
    return pl.pallas_call(
        _ret_kernel,
        grid=(nb, nc),
        in_specs=[pl.BlockSpec((lc, 2 * HR * DKR), lambda b, c: (b * nc + c, 0)),
                  pl.BlockSpec((lc, w), lambda b, c: (b * nc + c, 0)),
                  pl.BlockSpec((lc, w), lambda b, c: (b * nc + c, 0)),
                  pl.BlockSpec((None, HR, DKR, DVR), lambda b, c: (b, 0, 0, 0)),
                  _full_spec(dq.shape), _full_spec(dk.shape), _full_spec(dm.shape),
                  _full_spec(ds.shape), _full_spec(grn.shape)],
        out_specs=[pl.BlockSpec((lc, w), lambda b, c: (b * nc + c, 0)),
                   pl.BlockSpec((None, HR, DKR, DVR), lambda b, c: (b, 0, 0, 0))],
        out_shape=[jax.ShapeDtypeStruct((nb * t, w), _BF),
                   jax.ShapeDtypeStruct((nb, HR, DKR, DVR), _F32)],
        scratch_shapes=[pltpu.VMEM((HR // 2, 2 * DKR, 2 * DVR), _F32)],
        compiler_params=_params("parallel", "arbitrary"),
        name="retention",
    )(qk, v, gr, s0, dq, dk, dm, ds, grn)


def _merge(h, oa, ob, oc, wg, wpa, wpb, wpc, path):
    rows, d = h.shape
    tm = path.tm(512)
    tn = _tile(d, 512, LANE)
    nj = d // tn
    wo = oa.shape[1]
    o_spec = pl.BlockSpec((tm, wo), lambda i, j: (i, 0))
    wp_spec = pl.BlockSpec((wo, tn), lambda i, j: (0, j))
    return pl.pallas_call(
        _merge_kernel,
        grid=(rows // tm, nj),
        in_specs=[pl.BlockSpec((tm, d), lambda i, j: (i, 0)), o_spec, o_spec, o_spec,
                  pl.BlockSpec((d, tn), lambda i, j: (0, j)),
                  pl.BlockSpec((d, tn), lambda i, j: (0, nj + j)),
                  pl.BlockSpec((d, tn), lambda i, j: (0, 2 * nj + j)),
                  wp_spec, wp_spec, wp_spec],
        out_specs=pl.BlockSpec((tm, tn), lambda i, j: (i, j)),
        out_shape=jax.ShapeDtypeStruct((rows, d), _BF),
        compiler_params=_params("parallel", "arbitrary"),
        name="gated_merge",
    )(h, oa, ob, oc, wg, wg, wg, wpa, wpb, wpc)


def _out_proj(mg, wo, x, gt, lng, lnb, sc, sh, path, alpha):
    rows, d = x.shape
    tm = path.tm(512)
    gt, sc, sh = path.mod(gt, tm), path.mod(sc, tm), path.mod(sh, tm)
    return pl.pallas_call(
        functools.partial(_out_kernel, alpha=alpha),
        grid=(rows // tm,),
        in_specs=[_row_spec(tm, d), _full_spec(wo.shape), _row_spec(tm, d),
                  path.mod_spec(gt, tm, 1), _full_spec(lng.shape), _full_spec(lnb.shape),
                  path.mod_spec(sc, tm, 1), path.mod_spec(sh, tm, 1)],
        out_specs=[_row_spec(tm, d), _row_spec(tm, d)],
        out_shape=[jax.ShapeDtypeStruct((rows, d), _F32), jax.ShapeDtypeStruct((rows, d), _BF)],
        compiler_params=_params("parallel"),
        name="out_proj_ln",
    )(mg, wo, x, gt, lng, lnb, sc, sh)


def _ffn(h2, x1, prev, wa, wb, cw, cb, wd, gt, lng, lnb, sc, sh, path, alpha):
    rows, d = x1.shape
    dff = wa.shape[1]
    tm = path.tm(256 if path.dense else 512)
    tf = _tile(dff, 512, LANE)
    nm, nf = rows // tm, dff // tf
    gt, sc, sh = path.mod(gt, tm), path.mod(sc, tm), path.mod(sh, tm)
    row2 = pl.BlockSpec((tm, d), lambda i, j: (i, 0))
    common_in = [row2,
                 pl.BlockSpec((d, tf), lambda i, j: (0, j)),
                 pl.BlockSpec((d, tf), lambda i, j: (0, j)),
                 pl.BlockSpec((CONV_W, tf), lambda i, j: (0, j)),
                 pl.BlockSpec((1, tf), lambda i, j: (0, j)),
                 pl.BlockSpec((tf, d), lambda i, j: (j, 0)),
                 path.mod_spec(gt, tm, 2), pl.BlockSpec((1, d), lambda i, j: (0, 0)),
                 pl.BlockSpec((1, d), lambda i, j: (0, 0)),
                 path.mod_spec(sc, tm, 2), path.mod_spec(sh, tm, 2)]
    common_args = [x1, wa, wb, cw, cb, wd, gt, lng, lnb, sc, sh]
    if path.dense:
        seg = path.t
        rs = jnp.arange(rows) % seg
        sid = jnp.arange(rows) // seg
        p1 = jnp.where((rs == 0)[:, None], prev[sid, 1], 0.0)
        p2 = jnp.where((rs < 2)[:, None], prev[sid, jnp.minimum(rs, 1)], 0.0)
        tail = tm
        kern = functools.partial(_ffn_seg_kernel, seg=seg, alpha=alpha)
        in_specs = [row2, pl.BlockSpec((tm, tf), lambda i, j: (i, j)),
                    pl.BlockSpec((tm, tf), lambda i, j: (i, j))] + common_in
        args = [h2, p1, p2] + common_args
    else:
        hr = 16
        tail = 8
        kern = functools.partial(_ffn_halo_kernel, tiles_per_seq=path.t // tm, alpha=alpha)
        in_specs = [row2, pl.BlockSpec((hr, d), lambda i, j: (jnp.maximum(i * (tm // hr) - 1, 0), 0))
                    ] + common_in
        args = [h2, h2] + common_args
    y, hn, at = pl.pallas_call(
        kern,
        grid=(nm, nf),
        in_specs=in_specs,
        out_specs=[row2, row2, pl.BlockSpec((None, tail, tf), lambda i, j: (i, 0, j))],
        out_shape=[jax.ShapeDtypeStruct((rows, d), _F32), jax.ShapeDtypeStruct((rows, d), _BF),
                   jax.ShapeDtypeStruct((nm, tail, dff), _F32)],
        scratch_shapes=[pltpu.VMEM((tm, d), _F32)] + ([] if path.dense else (
            [pltpu.VMEM((tm + FFN_HALO, tf // FFN_GROUPS), _F32)] * FFN_GROUPS
            + [pltpu.VMEM((tm, tf // FFN_GROUPS), _F32)] * FFN_GROUPS
            + [pltpu.VMEM((tm, tf // FFN_GROUPS), _BF)] * FFN_GROUPS)),
        compiler_params=_params("parallel", "arbitrary"),
        name="conv_ffn_ln",
    )(*args)
    if path.dense:
        conv_new = at.reshape(path.nb, path.t, dff)[:, path.t - (CONV_W - 1):]
    else:
        tps = path.t // tm
        conv_new = at.reshape(path.nb, tps, tail, dff)[:, tps - 1, tail - (CONV_W - 1):]
    return y, hn, conv_new


def _layer(x, h, path, mods, nxt, cache, w, alpha):
    rows, d = x.shape
    nb, t = path.nb, path.t
    sh1, sc1, gt1, sh2, sc2, gt2 = mods
    cos, sin = path.rope_tables()
    tm_big = path.tm(1024)

    q_cat, ckv, kr128 = _latent(h, w["w_lat"], w["g_q"], w["g_kv"], w["w_uq"], cos, sin, path)
    if cache is None:
        k_t, v_a = _kvup_t(ckv, kr128, w["w_uk"].T, w["w_uv"], _mla_tiles(t)[1])
        oa = _mla_prompt(q_cat, k_t, v_a, nb, t)
    else:
        k_cat, v_a = _kvup(ckv, kr128, w["w_uk"], w["w_uv"])
        past = cache["ckv"].shape[2]
        kc, vc = _kvup(cache["ckv"].reshape(-1, KV_LORA), cache["krope"], w["w_uk"], w["w_uv"],
                       rows=nb * past, row_off=cache["layer"] * nb * past)
        q_pos = (path.pos0 + np.arange(t))[:, None] // CHUNK
        ok1 = (np.arange(past)[None, :] // CHUNK) <= q_pos
        ok2 = ((past + np.arange(t))[None, :] // CHUNK) <= q_pos
        t1 = jnp.asarray(np.where(ok1, 0.0, NEG)[None], _F32)
        t2 = jnp.asarray(np.where(ok2, 0.0, NEG)[None], _F32)
        oa = _attn2(q_cat.reshape(nb, t, -1), kc.reshape(nb, past, -1), vc.reshape(nb, past, -1),
                    k_cat.reshape(nb, t, -1), v_a.reshape(nb, t, -1), t1, t2,
                    QK_CAT, DVA, V_CAT, 0, 0, 0, "mla_attention_cached")
        oa = oa.reshape(rows, HA * DVA)

    wb3 = HB * DHB
    qkv = _mm(h, w["w_band"], _BF, tm_big, wb3, "band_qkv", first_block_scale=DHB ** -0.5 * LOG2E)
    if cache is None:
        ob = _band_prompt(qkv, w["rel_bias"], nb, t)
        keep = min(BAND_PREV * CHUNK, t)
        tk = _tile(keep, 512)
        per = keep // tk
        kv_new = _mm(h, w["w_band"][:, wb3:], _F32, tk, _tile(2 * wb3, 1024, LANE), "band_kv_tail",
                     row_map=lambda i: (i // per) * (t // tk) + (t - keep) // tk + i % per,
                     out_rows=nb * keep)
        bk_new = kv_new[:, :wb3].reshape(nb, keep, HB, DHB)
        bv_new = kv_new[:, wb3:].reshape(nb, keep, HB, DHB)
    else:
        kv_new = _mm(h, w["w_band"][:, wb3:], _F32, tm_big, _tile(2 * wb3, 1024, LANE), "band_kv_tail")
        bk_new = kv_new[:, :wb3].reshape(nb, t, HB, DHB)
        bv_new = kv_new[:, wb3:].reshape(nb, t, HB, DHB)
        nk = cache["band_k"].shape[2]
        t1 = _rel_table(w["rel_bias"], nk, t, nk, (path.pos0 - nk + np.arange(nk) >= 0)[None, :])
        t2 = _rel_table(w["rel_bias"], 0, t, t, np.ones((t, t), bool))
        qkv3 = qkv.reshape(nb, t, 3 * wb3)
        ob = _attn2(qkv3, cache["band_k"], cache["band_v"], qkv3, qkv3, t1, t2, DHB, DHB, DHB,
                    0, 1, 2, "band_attention_cached", layer=cache["layer"])
        ob = ob.reshape(rows, wb3)

    tm_r = path.tm(1024)
    qk_r = pl.pallas_call(
        _retqk_kernel,
        grid=(rows // tm_r,),
        in_specs=[_row_spec(tm_r, d), _full_spec(w["w_retqk"].shape),
                  path.rope_spec(cos, tm_r, 1), path.rope_spec(sin, tm_r, 1)],
        out_specs=_row_spec(tm_r, 2 * HR * DKR),
        out_shape=jax.ShapeDtypeStruct((rows, 2 * HR * DKR), _F32),
        compiler_params=_params("parallel"),
        name="ret_qk_rope",
    )(h, w["w_retqk"], cos, sin)
    v_r = _mm(h, w["w_retv"], _BF, tm_big, HR * DVR, "ret_v")
    g_r = _mm(h, w["w_retg"], _F32, tm_big, HR * DVR, "ret_gate")
    s0 = jnp.zeros((nb, HR, DKR, DVR), _F32) if cache is None else cache["ret"]
    oc, s_new = _retention(qk_r, v_r, g_r, s0, w["g_rn"], nb, t, 256)

    mg = _merge(h, oa, ob, oc, w["w_gates"], w["w_pa"], w["w_pb"], w["w_pc"], path)
    x1, h2 = _out_proj(mg, w["w_o"], x, gt1, w["ln1_g"], w["ln1_b"], sc2, sh2, path, alpha)

    prev = None if cache is None else cache["conv"]
    x2, hn, conv_new = _ffn(h2, x1, prev, w["w_fa"], w["w_fb"], w["cw"], w["cb"], w["w_fd"],
                            gt2, w["ln2_g"], w["ln2_b"], nxt[0], nxt[1], path, alpha)
    state = (ckv.reshape(nb, t, KV_LORA), kr128[:, :DR].reshape(nb, t, DR), bk_new, bv_new,
             s_new, conv_new)
    return x2, hn, state


def _layer_weights(l, wi, w_uq, w_ukv, w_ada_unused, p):
    d = wi.shape[1]
    o = np.cumsum([0, Q_LORA, KV_LORA, DR, HB * DHB, HB * DHB, HB * DHB, HR * DKR, HR * DKR,
                   HR * DVR, HR * DVR, d, d, d]).tolist()
    wl = wi[l]
    lat = jnp.concatenate([wl[:, :o[3]], jnp.zeros((d, LANE - DR), _BF)], axis=1)
    uq = w_uq[l].reshape(Q_LORA, HA, DN + DR)
    uq = jnp.pad(uq, ((0, 0), (0, 0), (0, QK_CAT - DN - DR))).reshape(Q_LORA, HA * QK_CAT)
    ukv = w_ukv[l].reshape(KV_LORA, HA, DN + DVA)
    row = lambda a: a[l][None, :]
    return {
        "w_lat": lat, "w_uq": uq,
        "w_uk": ukv[:, :, :DN].reshape(KV_LORA, HA * DN),
        "w_uv": ukv[:, :, DN:].reshape(KV_LORA, HA * DVA),
        "w_band": wl[:, o[3]:o[6]], "w_retqk": wl[:, o[6]:o[8]], "w_retv": wl[:, o[8]:o[9]],
        "w_retg": wl[:, o[9]:o[10]], "w_gates": wl[:, o[10]:o[13]],
        "g_q": row(p["g_q_lora"]), "g_kv": row(p["g_kv_lora"]), "rel_bias": p["rel_bias"][l],
        "g_rn": row(p["g_ret_norm"]),
        "w_pa": p["w_branch_a"][l], "w_pb": p["w_branch_b"][l], "w_pc": p["w_branch_c"][l],
        "w_o": p["w_o"][l], "ln1_g": row(p["ln1_g"]), "ln1_b": row(p["ln1_b"]),
        "w_fa": p["w_ff_a"][l], "w_fb": p["w_ff_b"][l], "cw": p["conv_w"][l],
        "cb": row(p["conv_b"]), "w_fd": p["w_ff_down"][l],
        "ln2_g": row(p["ln2_g"]), "ln2_b": row(p["ln2_b"]),
    }


def kernel(x_prompt, x_sample, c_prompt, c_sample, cache_mla_ckv, cache_mla_krope, cache_band_k, cache_band_v, state_ret, state_conv, w_ada, b_ada, w_in, g_q_lora, g_kv_lora, w_uq, w_ukv, rel_bias, g_ret_norm, w_branch_a, w_branch_b, w_branch_c, w_o, ln1_g, ln1_b, w_ff_a, w_ff_b, conv_w, conv_b, w_ff_down, ln2_g, ln2_b):
    nb_p, t_p, d = x_prompt.shape
    nb_s, t_s, _ = x_sample.shape
    depth = w_in.shape[0]
    past = cache_mla_ckv.shape[2]
    alpha = (2 * depth) ** 0.25

    nc = nb_p + nb_s
    ncp = -(-nc // 16) * 16
    c_all = jnp.pad(jnp.concatenate([c_prompt, c_sample], 0), ((0, ncp - nc), (0, 0)))
    tn_a = _tile(6 * d, 2048, LANE)
    ada = pl.pallas_call(
        _ada_kernel,
        grid=(depth, 6 * d // tn_a),
        in_specs=[pl.BlockSpec((ncp, d), lambda l, j: (0, 0)),
                  pl.BlockSpec((None, d, tn_a), lambda l, j: (l, 0, j)),
                  pl.BlockSpec((None, 1, tn_a), lambda l, j: (l, 0, j))],
        out_specs=pl.BlockSpec((None, ncp, tn_a), lambda l, j: (l, 0, j)),
        out_shape=jax.ShapeDtypeStruct((depth, ncp, 6 * d), _F32),
        compiler_params=_params("parallel", "arbitrary"),
        name="ada_ln",
    )(c_all, w_ada.astype(_BF), b_ada[:, None, :])

    def mods(l, lo, n):
        a = ada[l, lo:lo + n]
        sh1, sc1, gt1, sh2, sc2, gt2 = [a[:, k * d:(k + 1) * d] for k in range(6)]
        return (sh1, 1.0 + sc1, 1.0 + gt1, sh2, 1.0 + sc2, 1.0 + gt2)

    bf = lambda a: a.astype(_BF)
    p = dict(g_q_lora=g_q_lora, g_kv_lora=g_kv_lora, rel_bias=rel_bias, g_ret_norm=g_ret_norm,
             w_branch_a=bf(w_branch_a), w_branch_b=bf(w_branch_b), w_branch_c=bf(w_branch_c),
             w_o=bf(w_o), ln1_g=ln1_g, ln1_b=ln1_b, w_ff_a=bf(w_ff_a), w_ff_b=bf(w_ff_b),
             conv_w=conv_w, conv_b=conv_b, w_ff_down=bf(w_ff_down), ln2_g=ln2_g, ln2_b=ln2_b)
    wi, wq, wkv = bf(w_in), bf(w_uq), bf(w_ukv)
    krope_cache = jnp.pad(cache_mla_krope.reshape(-1, DR), ((0, 0), (0, LANE - DR)))

    streams = [(_Path(nb_p, t_p, 0, False), x_prompt, 0),
               (_Path(nb_s, t_s, past, True), x_sample, nb_p)]
    outs = []
    for path, x0, lo in streams:
        x = x0.reshape(path.rows, d)
        m0 = mods(0, lo, path.nb)
        h = _modulate(x, m0[1], m0[0], path)
        states = []
        for l in range(depth):
            m = mods(l, lo, path.nb)
            mn = mods(min(l + 1, depth - 1), lo, path.nb)
            cache = None
            if path.dense:
                cache = dict(layer=l, ckv=cache_mla_ckv, krope=krope_cache, band_k=cache_band_k,
                             band_v=cache_band_v, ret=state_ret[l], conv=state_conv[l])
            w = _layer_weights(l, wi, wq, wkv, None, p)
            x, h, st = _layer(x, h, path, m, (mn[1], mn[0]), cache, w, alpha)
            states.append(st)
        outs.append((x.reshape(path.nb, path.t, d), [jnp.stack(z) for z in zip(*states)]))
    (y_p, st_p), (y_s, st_s) = outs
    return (y_p, y_s, *st_p, *st_s)
```

```python
import functools

import numpy as np
import jax
import jax.numpy as jnp
from jax import lax
from jax.experimental import pallas as pl
from jax.experimental.pallas import tpu as pltpu

CHUNK = 64
HA, DN, DR, DVA = 8, 128, 64, 128
Q_LORA, KV_LORA = 512, 256
HB, DHB = 8, 128
BAND_PREV, REL_CLIP = 8, 128
HR, DKR, DVR = 8, 64, 128
CONV_W = 3
ROPE_THETA = 10000.0
EPS = 1e-5
LANE = 128
QK_CAT = DN + 2 * DR
NEG = -1e30
VMEM_LIMIT = 56 * 1024 * 1024
LOG2E = float(np.log2(np.e))
MLA_QSCALE = (DN + DR) ** -0.5 * LOG2E
V_CAT = 2 * DVA
FFN_HALO = 8
FFN_ROWS = 64
FFN_GROUPS = 2

_BF = jnp.bfloat16
_F32 = jnp.float32


def _dot(a, b):
    return jnp.dot(a, b, preferred_element_type=_F32)


def _dot_nt(a, b):
    return lax.dot_general(a, b, (((1,), (1,)), ((), ())), preferred_element_type=_F32)


def _dot_tn(a, b):
    return lax.dot_general(a, b, (((0,), (0,)), ((), ())), preferred_element_type=_F32)


def _tile(n, pref, step=8):
    if n <= pref:
        return n
    t = pref - pref % step
    while n % t:
        t -= step
    return t


def _params(*sem):
    return pltpu.CompilerParams(dimension_semantics=sem, vmem_limit_bytes=VMEM_LIMIT)


def _swap32(x):
    n = x.shape[-1]
    lane = lax.broadcasted_iota(jnp.int32, x.shape, 1)
    fwd = pltpu.roll(x, n - DR // 2, 1)
    bwd = pltpu.roll(x, DR // 2, 1)
    return jnp.where((lane % DR) < DR // 2, fwd, bwd)


def _rope128(x, cos, sin):
    return x * cos + _swap32(x) * sin


def _layer_norm(z, g, b):
    mu = jnp.mean(z, -1, keepdims=True)
    zc = z - mu
    var = jnp.mean(zc * zc, -1, keepdims=True)
    return zc * lax.rsqrt(var + EPS) * g + b


def _rms(x, g):
    return x * lax.rsqrt(jnp.mean(x * x, -1, keepdims=True) + EPS) * g


def _ada_kernel(c_ref, w_ref, b_ref, o_ref):
    c = c_ref[...]
    s = (c * jax.nn.sigmoid(c)).astype(_BF)
    o_ref[...] = _dot(s, w_ref[...]) + b_ref[...]


def _mod_kernel(x_ref, sc_ref, sh_ref, o_ref):
    o_ref[...] = (x_ref[...] * sc_ref[...] + sh_ref[...]).astype(o_ref.dtype)


def _mm_kernel(x_ref, w_ref, o_ref, *, first_block_scale):
    y = _dot(x_ref[...], w_ref[...])
    if first_block_scale is not None:
        y = y * jnp.where(pl.program_id(1) == 0, first_block_scale, 1.0)
    o_ref[...] = y.astype(o_ref.dtype)


def _retqk_kernel(x_ref, w_ref, cos_ref, sin_ref, o_ref):
    y = _dot(x_ref[...], w_ref[...])
    cos, sin = cos_ref[...], sin_ref[...]
    nq = HR * DKR // LANE
    for c in range(2 * nq):
        blk = _rope128(y[:, c * LANE:(c + 1) * LANE], cos, sin)
        if c < nq:
            blk = blk * (DKR ** -0.5)
        o_ref[:, c * LANE:(c + 1) * LANE] = blk


def _lat_kernel(x_ref, wl_ref, gq_ref, gkv_ref, wuq_ref, cos_ref, sin_ref,
                q_ref, ckv_ref, kr_ref):
    y = _dot(x_ref[...], wl_ref[...])
    cos, sin = cos_ref[...], sin_ref[...]
    cqn = _rms(y[:, :Q_LORA], gq_ref[...])
    ckv_ref[...] = _rms(y[:, Q_LORA:Q_LORA + KV_LORA], gkv_ref[...])
    kr_ref[...] = _rope128(y[:, Q_LORA + KV_LORA:], cos, sin)
    qa = _dot(cqn.astype(_BF), wuq_ref[...]) * MLA_QSCALE
    for h in range(HA):
        o = h * QK_CAT
        q_ref[:, o:o + DN] = qa[:, o:o + DN].astype(_BF)
        q_ref[:, o + DN:o + QK_CAT] = _rope128(qa[:, o + DN:o + QK_CAT], cos, sin).astype(_BF)


def _kvup_kernel(ckv_ref, kr_ref, wk_ref, wv_ref, k_ref, v_ref):
    c = ckv_ref[...].astype(_BF)
    kn = _dot(c, wk_ref[...])
    vn = _dot(c, wv_ref[...])
    krb = kr_ref[...].astype(_BF)
    ones = jnp.ones((c.shape[0], DVA), _BF)
    for h in range(HA):
        o = h * QK_CAT
        k_ref[:, o:o + DN] = kn[:, h * DN:(h + 1) * DN].astype(_BF)
        k_ref[:, o + DN:o + QK_CAT] = krb
        v_ref[:, h * V_CAT:h * V_CAT + DVA] = vn[:, h * DVA:(h + 1) * DVA].astype(_BF)
        v_ref[:, h * V_CAT + DVA:(h + 1) * V_CAT] = ones


def _kvup_t_kernel(ckv_ref, kr_ref, wkt_ref, wv_ref, kt_ref, v_ref):
    c = ckv_ref[...].astype(_BF)
    knt = _dot_nt(wkt_ref[...], c)
    vn = _dot(c, wv_ref[...])
    krt = kr_ref[...].T.astype(_BF)
    ones = jnp.ones((c.shape[0], DVA), _BF)
    for h in range(HA):
        o = h * QK_CAT
        kt_ref[o:o + DN, :] = knt[h * DN:(h + 1) * DN, :].astype(_BF)
        kt_ref[o + DN:o + QK_CAT, :] = krt
        v_ref[:, h * V_CAT:h * V_CAT + DVA] = vn[:, h * DVA:(h + 1) * DVA].astype(_BF)
        v_ref[:, h * V_CAT + DVA:(h + 1) * V_CAT] = ones


def _mla_kernel(q_ref, kt_ref, v_ref, o_ref, s_scr, p_scr, m_scr, a_scr, acc_scr, *, hg, tq, tk):
    qi = pl.program_id(2)
    kc = tk // CHUNK
    m_scr[...] = jnp.full(m_scr.shape, NEG, _F32)
    acc_scr[...] = jnp.zeros(acc_scr.shape, _F32)
    col = lax.broadcasted_iota(jnp.int32, (CHUNK, tk), 1)

    def scores(kt, slot, row0=0):
        for h in range(hg):
            s_scr[slot, h, row0:, :] = _dot(q_ref[row0:, h * QK_CAT:(h + 1) * QK_CAT],
                                            kt_ref[kt, h * QK_CAT:(h + 1) * QK_CAT, :])

    def consume(kt, slot, diag):
        start = pl.multiple_of(kt * tk, tk)
        row0 = 0 if diag is None else diag * tk
        for h in range(hg):
            for r in range(row0 // CHUNK, tq // CHUNK):
                rows = slice(r * CHUNK, (r + 1) * CHUNK)
                ncol = tk if diag is None else min((r - diag * kc + 1) * CHUNK, tk)
                s = s_scr[slot, h, rows, :]
                if ncol < tk:
                    s = jnp.where(col < ncol, s, NEG)
                m_old = m_scr[h, rows, :]
                m_new = jnp.maximum(m_old, jnp.max(s, -1, keepdims=True))
                p_scr[h, rows, :] = jnp.exp2(s - m_new).astype(_BF)
                a_scr[h, rows, :] = jnp.exp2(m_old - m_new)
                m_scr[h, rows, :] = m_new
            acc_scr[h, row0:, :] = a_scr[h, row0:, :] * acc_scr[h, row0:, :] + _dot(
                p_scr[h, row0:, :], v_ref[pl.ds(start, tk), h * V_CAT:(h + 1) * V_CAT])

    def pair(j, c):
        scores(2 * j + 1, 1)
        consume(2 * j, 0, None)
        scores(2 * j + 2, 0)
        consume(2 * j + 1, 1, None)
        return c

    scores(0, 0)
    lax.fori_loop(0, qi, pair, 0)
    scores(2 * qi + 1, 1, tk)
    consume(2 * qi, 0, 0)
    consume(2 * qi + 1, 1, 1)
    for h in range(hg):
        acc = acc_scr[h]
        o_ref[:, h * DVA:(h + 1) * DVA] = (acc[:, :DVA] / acc[:, DVA:]).astype(o_ref.dtype)


def _band_kernel(q_ref, k_ref, v_ref, tb_ref, o_ref, *, hg, win, chunks_per_tile):
    n = pl.program_id(2)
    ws = pl.multiple_of(jnp.maximum(n * chunks_per_tile - BAND_PREV, 0) * CHUNK, CHUNK)
    for h in range(hg):
        q = q_ref[:, h * DHB:(h + 1) * DHB]
        k = k_ref[pl.ds(ws, win), h * DHB:(h + 1) * DHB]
        v = v_ref[pl.ds(ws, win), h * DHB:(h + 1) * DHB]
        s = _dot_nt(q, k) + tb_ref[h]
        p = jnp.exp2(s - jnp.max(s, -1, keepdims=True))
        l = jnp.sum(p, -1, keepdims=True)
        o_ref[:, h * DHB:(h + 1) * DHB] = (_dot(p.astype(_BF), v) / l).astype(o_ref.dtype)


def _attn2_kernel(q_ref, k1_ref, v1_ref, k2_ref, v2_ref, t1_ref, t2_ref, o_ref,
                  *, heads, dq, dv, v_stride, per_head_table):
    for h in range(heads):
        th = h if per_head_table else 0
        q = q_ref[:, h * dq:(h + 1) * dq]
        k1 = k1_ref[:, h * dq:(h + 1) * dq].astype(_BF)
        v1 = v1_ref[:, h * v_stride:h * v_stride + dv].astype(_BF)
        k2 = k2_ref[:, h * dq:(h + 1) * dq].astype(_BF)
        v2 = v2_ref[:, h * v_stride:h * v_stride + dv].astype(_BF)
        s1 = _dot_nt(q, k1) + t1_ref[th]
        s2 = _dot_nt(q, k2) + t2_ref[th]
        m = jnp.maximum(jnp.max(s1, -1, keepdims=True), jnp.max(s2, -1, keepdims=True))
        p1 = jnp.exp2(s1 - m)
        p2 = jnp.exp2(s2 - m)
        l = jnp.sum(p1, -1, keepdims=True) + jnp.sum(p2, -1, keepdims=True)
        o = _dot(p1.astype(_BF), v1) + _dot(p2.astype(_BF), v2)
        o_ref[:, h * dv:(h + 1) * dv] = (o / l).astype(o_ref.dtype)


def _band_cached_kernel(qkv_ref, k1_ref, v1_ref, t1_ref, t2_ref, o_ref, *, heads, d):
    tq = qkv_ref.shape[0]
    stack = lambda off: jnp.concatenate(
        [qkv_ref[:, off + h * d:off + (h + 1) * d] for h in range(heads)], axis=0)
    q, k2, v2 = stack(0), stack(heads * d), stack(2 * heads * d)
    s1 = _dot_nt(q, k1_ref[...].astype(_BF)) + t1_ref[...]
    s2 = _dot_nt(q, k2) + t2_ref[...]
    m = jnp.maximum(jnp.max(s1, -1, keepdims=True), jnp.max(s2, -1, keepdims=True))
    p1 = jnp.exp2(s1 - m)
    p2 = jnp.exp2(s2 - m)
    l = jnp.sum(p1, -1, keepdims=True) + jnp.sum(p2, -1, keepdims=True)
    o = (_dot(p1.astype(_BF), v1_ref[...].astype(_BF)) + _dot(p2.astype(_BF), v2)) / l
    for h in range(heads):
        o_ref[:, h * d:(h + 1) * d] = o[h * tq:(h + 1) * tq, :].astype(o_ref.dtype)


def _ret_kernel(qk_ref, v_ref, gr_ref, s0_ref, dq_ref, dk_ref, dm_ref, ds_ref, grn_ref,
                o_ref, sout_ref, s_scr):
    ci = pl.program_id(1)
    npair = HR // 2
    r2 = lax.broadcasted_iota(jnp.int32, (2 * DKR, 2 * DVR), 0) // DKR
    c2 = lax.broadcasted_iota(jnp.int32, (2 * DKR, 2 * DVR), 1) // DVR
    on_diag = r2 == c2

    @pl.when(ci == 0)
    def _():
        for p in range(npair):
            top = jnp.concatenate([s0_ref[2 * p], jnp.zeros((DKR, DVR), _F32)], axis=1)
            bot = jnp.concatenate([jnp.zeros((DKR, DVR), _F32), s0_ref[2 * p + 1]], axis=1)
            s_scr[p] = jnp.concatenate([top, bot], axis=0)

    lane = lax.broadcasted_iota(jnp.int32, (qk_ref.shape[0], LANE), 1)
    for p in range(npair):
        q2 = qk_ref[:, p * LANE:(p + 1) * LANE]
        k2 = qk_ref[:, HR * DKR + p * LANE:HR * DKR + (p + 1) * LANE]
        v2 = v_ref[:, 2 * p * DVR:(2 * p + 2) * DVR]
        k2b = k2.astype(_BF)
        s2 = s_scr[p]
        cross = _dot((q2 * dq_ref[p]).astype(_BF), s2.astype(_BF))
        for e in range(2):
            h = 2 * p + e
            qm = jnp.where((lane // DKR) == e, q2, 0.0).astype(_BF)
            inner = _dot_nt(qm, k2b) * dm_ref[h]
            o = _dot(inner.astype(_BF), v2[:, e * DVR:(e + 1) * DVR])
            o = o + cross[:, e * DVR:(e + 1) * DVR]
            mu = jnp.mean(o, -1, keepdims=True)
            oc = o - mu
            var = jnp.mean(oc * oc, -1, keepdims=True)
            on = oc * lax.rsqrt(var + EPS) * grn_ref[:, h * DVR:(h + 1) * DVR]
            g = gr_ref[:, h * DVR:(h + 1) * DVR]
            o_ref[:, h * DVR:(h + 1) * DVR] = (g * jax.nn.sigmoid(g) * on).astype(o_ref.dtype)
        upd = _dot_tn((k2 * dk_ref[p]).astype(_BF), v2)
        s_scr[p] = s2 * ds_ref[p] + jnp.where(on_diag, upd, 0.0)

    @pl.when(ci == pl.num_programs(1) - 1)
    def _():
        for p in range(npair):
            sp = s_scr[p]
            sout_ref[2 * p] = sp[:DKR, :DVR]
            sout_ref[2 * p + 1] = sp[DKR:, DVR:]


def _merge_kernel(h_ref, oa_ref, ob_ref, oc_ref, wga_ref, wgb_ref, wgc_ref,
                  wpa_ref, wpb_ref, wpc_ref, o_ref):
    h = h_ref[...]

    def branch(o_r, wg_r, wp_r):
        return jax.nn.sigmoid(_dot(h, wg_r[...])) * _dot(o_r[...], wp_r[...])

    m = branch(oa_ref, wga_ref, wpa_ref) + branch(ob_ref, wgb_ref, wpb_ref)
    m = m + branch(oc_ref, wgc_ref, wpc_ref)
    o_ref[...] = m.astype(o_ref.dtype)


def _out_kernel(mg_ref, wo_ref, x_ref, gt_ref, lng_ref, lnb_ref, sc_ref, sh_ref,
                x1_ref, h2_ref, *, alpha):
    y = _dot(mg_ref[...], wo_ref[...])
    x1 = _layer_norm(alpha * x_ref[...] + gt_ref[...] * y, lng_ref[...], lnb_ref[...])
    x1_ref[...] = x1
    h2_ref[...] = (x1 * sc_ref[...] + sh_ref[...]).astype(h2_ref.dtype)


def _gated(a, a1, a2, bb, cw, cb):
    conv = cb + a2 * cw[0:1, :] + a1 * cw[1:2, :] + a * cw[2:3, :]
    gelu = 0.5 * conv * (1.0 + lax.erf(conv * (2.0 ** -0.5)))
    return (gelu * bb).astype(_BF)


def _ffn_clear(acc_ref):
    @pl.when(pl.program_id(1) == 0)
    def _():
        acc_ref[...] = jnp.zeros(acc_ref.shape, _F32)


def _ffn_norm(x_ref, gt_ref, lng_ref, lnb_ref, sc_ref, sh_ref, y_ref, hn_ref, acc_ref, alpha):
    @pl.when(pl.program_id(1) == pl.num_programs(1) - 1)
    def _():
        z = alpha * x_ref[...] + gt_ref[...] * acc_ref[...]
        y = _layer_norm(z, lng_ref[...], lnb_ref[...])
        y_ref[...] = y
        hn_ref[...] = (y * sc_ref[...] + sh_ref[...]).astype(hn_ref.dtype)


def _ffn_halo_kernel(h_ref, halo_ref, x_ref, wa_ref, wb_ref, cw_ref, cb_ref, wd_ref, gt_ref,
                     lng_ref, lnb_ref, sc_ref, sh_ref, y_ref, hn_ref, at_ref, acc_ref,
                     *scr, tiles_per_seq, alpha):
    ng = len(scr) // 3
    a_scr, b_scr, g_scr = scr[:ng], scr[ng:2 * ng], scr[2 * ng:]
    i = pl.program_id(0)
    h = h_ref[...]
    halo = halo_ref[...]
    tm = h.shape[0]
    w = wa_ref.shape[1] // ng
    keep = jnp.where(i % tiles_per_seq == 0, 0.0, 1.0)
    _ffn_clear(acc_ref)
    for c in range(ng):
        cs = slice(c * w, (c + 1) * w)
        a_scr[c][FFN_HALO:, :] = _dot(h, wa_ref[:, cs])
        b_scr[c][...] = _dot(h, wb_ref[:, cs])
        ah = _dot(halo, wa_ref[:, cs])
        a_scr[c][:FFN_HALO, :] = ah[ah.shape[0] - FFN_HALO:, :] * keep
        at_ref[:, cs] = a_scr[c][tm:, :]
    rc = min(FFN_ROWS, tm)
    for c in range(ng):
        cs = slice(c * w, (c + 1) * w)
        cw, cb = cw_ref[:, cs], cb_ref[:, cs]
        for k in range(tm // rc):
            r = FFN_HALO + k * rc
            g_scr[c][r - FFN_HALO:r - FFN_HALO + rc, :] = _gated(
                a_scr[c][r:r + rc, :], a_scr[c][r - 1:r - 1 + rc, :], a_scr[c][r - 2:r - 2 + rc, :],
                b_scr[c][r - FFN_HALO:r - FFN_HALO + rc, :], cw, cb)
        acc_ref[...] += _dot(g_scr[c][...], wd_ref[cs, :])
    _ffn_norm(x_ref, gt_ref, lng_ref, lnb_ref, sc_ref, sh_ref, y_ref, hn_ref, acc_ref, alpha)


def _ffn_seg_kernel(h_ref, p1_ref, p2_ref, x_ref, wa_ref, wb_ref, cw_ref, cb_ref, wd_ref,
                    gt_ref, lng_ref, lnb_ref, sc_ref, sh_ref, y_ref, hn_ref, at_ref, acc_ref,
                    *, seg, alpha):
    h = h_ref[...]
    a = _dot(h, wa_ref[...])
    bb = _dot(h, wb_ref[...])
    rs = lax.broadcasted_iota(jnp.int32, a.shape, 0) % seg
    a1 = jnp.where(rs == 0, p1_ref[...], pltpu.roll(a, 1, 0))
    a2 = jnp.where(rs < 2, p2_ref[...], pltpu.roll(a, 2, 0))
    at_ref[...] = a
    _ffn_clear(acc_ref)
    acc_ref[...] += _dot(_gated(a, a1, a2, bb, cw_ref[...], cb_ref[...]), wd_ref[...])
    _ffn_norm(x_ref, gt_ref, lng_ref, lnb_ref, sc_ref, sh_ref, y_ref, hn_ref, acc_ref, alpha)


class _Path:
    def __init__(self, nb, t, pos0, dense):
        self.nb, self.t, self.rows, self.pos0, self.dense = nb, t, nb * t, pos0, dense

    def tm(self, pref):
        return _tile(self.rows if self.dense else self.t, pref)

    def mod(self, v, tm):
        d = v.shape[-1]
        if self.dense:
            return jnp.repeat(v, self.t, axis=0).reshape(self.rows // tm, tm, d)
        return v[:, None, :]

    def mod_spec(self, arr, tm, rank):
        g, r, d = arr.shape
        tpg = (self.rows // tm) // g
        if rank == 1:
            return pl.BlockSpec((None, r, d), lambda i: (i // tpg, 0, 0))
        return pl.BlockSpec((None, r, d), lambda i, j: (i // tpg, 0, 0))

    def rope_tables(self):
        half = DR // 2
        inv = ROPE_THETA ** (-jnp.arange(half, dtype=_F32) / half)
        pos = self.pos0 + jnp.arange(self.t, dtype=jnp.int32)
        ang = pos.astype(_F32)[:, None] * inv[None, :]
        cos, sin = jnp.cos(ang), jnp.sin(ang)
        cos = jnp.concatenate([cos, cos, cos, cos], -1)
        sin = jnp.concatenate([-sin, sin, -sin, sin], -1)
        if self.dense:
            cos, sin = jnp.tile(cos, (self.nb, 1)), jnp.tile(sin, (self.nb, 1))
        return cos, sin

    def rope_spec(self, tab, tm, rank):
        nt = tab.shape[0] // tm
        if rank == 1:
            return pl.BlockSpec((tm, LANE), lambda i: (i % nt, 0))
        return pl.BlockSpec((tm, LANE), lambda i, j: (i % nt, 0))


def _row_spec(tm, n):
    return pl.BlockSpec((tm, n), lambda i: (i, 0))


def _full_spec(shape):
    nd = len(shape)
    return pl.BlockSpec(shape, lambda *_: (0,) * nd)


def _mm(x, w, out_dtype, tm, tn, name, row_map=None, out_rows=None, first_block_scale=None):
    rows, k = x.shape
    n = w.shape[1]
    out_rows = rows if out_rows is None else out_rows
    row_map = (lambda i: i) if row_map is None else row_map
    return pl.pallas_call(
        functools.partial(_mm_kernel, first_block_scale=first_block_scale),
        grid=(out_rows // tm, n // tn),
        in_specs=[pl.BlockSpec((tm, k), lambda i, j: (row_map(i), 0)),
                  pl.BlockSpec((k, tn), lambda i, j: (0, j))],
        out_specs=pl.BlockSpec((tm, tn), lambda i, j: (i, j)),
        out_shape=jax.ShapeDtypeStruct((out_rows, n), out_dtype),
        compiler_params=_params("parallel", "arbitrary"),
        name=name,
    )(x, w)


def _modulate(x, sc, sh, path):
    rows, d = x.shape
    tm = path.tm(1024)
    sc, sh = path.mod(sc, tm), path.mod(sh, tm)
    return pl.pallas_call(
        _mod_kernel,
        grid=(rows // tm,),
        in_specs=[_row_spec(tm, d), path.mod_spec(sc, tm, 1), path.mod_spec(sh, tm, 1)],
        out_specs=_row_spec(tm, d),
        out_shape=jax.ShapeDtypeStruct((rows, d), _BF),
        compiler_params=_params("parallel"),
        name="modulate",
    )(x, sc, sh)


def _latent(h, wl, gq, gkv, wuq, cos, sin, path):
    rows, d = h.shape
    tm = path.tm(512)
    nl = wl.shape[1]
    return pl.pallas_call(
        _lat_kernel,
        grid=(rows // tm,),
        in_specs=[_row_spec(tm, d), _full_spec(wl.shape), _full_spec(gq.shape),
                  _full_spec(gkv.shape), _full_spec(wuq.shape),
                  path.rope_spec(cos, tm, 1), path.rope_spec(sin, tm, 1)],
        out_specs=[_row_spec(tm, HA * QK_CAT), _row_spec(tm, KV_LORA), _row_spec(tm, LANE)],
        out_shape=[jax.ShapeDtypeStruct((rows, HA * QK_CAT), _BF),
                   jax.ShapeDtypeStruct((rows, KV_LORA), _F32),
                   jax.ShapeDtypeStruct((rows, LANE), _F32)],
        compiler_params=_params("parallel"),
        name="mla_latent",
    )(h, wl, gq, gkv, wuq, cos, sin)


def _kvup(ckv, kr, wk, wv, rows=None, row_off=0):
    rows = ckv.shape[0] if rows is None else rows
    tm = _tile(rows, 1024)
    off = row_off // tm
    return pl.pallas_call(
        _kvup_kernel,
        grid=(rows // tm,),
        in_specs=[pl.BlockSpec((tm, KV_LORA), lambda i: (off + i, 0)),
                  pl.BlockSpec((tm, LANE), lambda i: (off + i, 0)), _full_spec(wk.shape),
                  _full_spec(wv.shape)],
        out_specs=[_row_spec(tm, HA * QK_CAT), _row_spec(tm, HA * V_CAT)],
        out_shape=[jax.ShapeDtypeStruct((rows, HA * QK_CAT), _BF),
                   jax.ShapeDtypeStruct((rows, HA * V_CAT), _BF)],
        compiler_params=_params("parallel"),
        name="mla_kv_up",
    )(ckv, kr, wk, wv)


def _kvup_t(ckv, kr, wkt, wv, tk):
    rows = ckv.shape[0]
    return pl.pallas_call(
        _kvup_t_kernel,
        grid=(rows // tk,),
        in_specs=[_row_spec(tk, KV_LORA), _row_spec(tk, LANE), _full_spec(wkt.shape),
                  _full_spec(wv.shape)],
        out_specs=[pl.BlockSpec((None, HA * QK_CAT, tk), lambda i: (i, 0, 0)),
                   _row_spec(tk, HA * V_CAT)],
        out_shape=[jax.ShapeDtypeStruct((rows // tk, HA * QK_CAT, tk), _BF),
                   jax.ShapeDtypeStruct((rows, HA * V_CAT), _BF)],
        compiler_params=_params("parallel"),
        name="mla_kv_up_t",
    )(ckv, kr, wkt, wv)


def _mla_tiles(t):
    tk = _tile(t // 2, 512, CHUNK)
    assert t % (2 * tk) == 0
    return 2 * tk, tk


def _mla_prompt(q, kt, v, nb, t):
    hg = 2
    tq, tk = _mla_tiles(t)
    nq, nk = t // tq, t // tk
    kern = functools.partial(_mla_kernel, hg=hg, tq=tq, tk=tk)
    return pl.pallas_call(
        kern,
        grid=(nb, HA // hg, nq),
        in_specs=[pl.BlockSpec((tq, hg * QK_CAT), lambda b, g, i: (b * nq + i, g)),
                  pl.BlockSpec((nk, hg * QK_CAT, tk), lambda b, g, i: (b, g, 0)),
                  pl.BlockSpec((t, hg * V_CAT), lambda b, g, i: (b, g))],
        out_specs=pl.BlockSpec((tq, hg * DVA), lambda b, g, i: (b * nq + i, g)),
        out_shape=jax.ShapeDtypeStruct((nb * t, HA * DVA), _BF),
        scratch_shapes=[pltpu.VMEM((2, hg, tq, tk), _F32), pltpu.VMEM((hg, tq, tk), _BF),
                        pltpu.VMEM((hg, tq, 1), _F32), pltpu.VMEM((hg, tq, 1), _F32),
                        pltpu.VMEM((hg, tq, V_CAT), _F32)],
        compiler_params=_params("parallel", "parallel", "arbitrary"),
        name="mla_attention",
    )(q, kt, v)


def _band_tables(rel_bias, tq, t):
    r = tq // CHUNK
    win = (r + BAND_PREV) * CHUNK
    nv = -(-BAND_PREV // r) + 1
    i = np.arange(tq)[:, None]
    m = np.arange(win)[None, :]
    tabs = []
    for var in range(nv):
        n0 = var * r
        ws = max(n0 - BAND_PREV, 0)
        cq = n0 + i // CHUNK
        ck = ws + m // CHUNK
        allowed = (ck <= cq) & (ck >= cq - BAND_PREV)
        tabs.append(_rel_table(rel_bias, (n0 - ws) * CHUNK, tq, win, allowed))
    return jnp.stack(tabs), win, nv


def _rel_table(rel_bias, off, nq, nk, allowed):
    period = nq + nk
    u = np.arange(period)
    u = np.where(u < nk, u, u - period)
    pick = np.zeros((2 * REL_CLIP + 1, period), np.float32)
    pick[np.clip(off - u, -REL_CLIP, REL_CLIP) + REL_CLIP, np.arange(period)] = 1.0
    f = jnp.dot(rel_bias, pick, precision=lax.Precision.HIGHEST) * LOG2E
    circ = jnp.tile(f, (1, nq))[:, :nq * (period - 1)].reshape(-1, nq, period - 1)[:, :, :nk]
    return jnp.where(jnp.asarray(np.broadcast_to(allowed, (nq, nk)))[None], circ, NEG).astype(_F32)


def _band_prompt(qkv, rel_bias, nb, t):
    hg = 4
    tq = _tile(t, 256, CHUNK)
    nq = t // tq
    tabs, win, nv = _band_tables(rel_bias, tq, t)
    ng = HB // hg
    kern = functools.partial(_band_kernel, hg=hg, win=win, chunks_per_tile=tq // CHUNK)
    return pl.pallas_call(
        kern,
        grid=(nb, ng, nq),
        in_specs=[pl.BlockSpec((tq, hg * DHB), lambda b, g, i: (b * nq + i, g)),
                  pl.BlockSpec((t, hg * DHB), lambda b, g, i: (b, ng + g)),
                  pl.BlockSpec((t, hg * DHB), lambda b, g, i: (b, 2 * ng + g)),
                  pl.BlockSpec((None, hg, tq, win),
                               lambda b, g, i: (jnp.minimum(i, nv - 1), g, 0, 0))],
        out_specs=pl.BlockSpec((tq, hg * DHB), lambda b, g, i: (b * nq + i, g)),
        out_shape=jax.ShapeDtypeStruct((nb * t, HB * DHB), _BF),
        compiler_params=_params("parallel", "parallel", "arbitrary"),
        name="band_attention",
    )(qkv, qkv, qkv, tabs)


def _attn2(q, k1, v1, k2, v2, t1, t2, dq, dv, v_stride, name):
    nb, tq, _ = q.shape
    heads = HA
    n1 = k1.shape[1]
    kern = functools.partial(_attn2_kernel, heads=heads, dq=dq, dv=dv, v_stride=v_stride,
                             per_head_table=t1.shape[0] > 1)
    return pl.pallas_call(
        kern,
        grid=(nb,),
        in_specs=[pl.BlockSpec((None, tq, heads * dq), lambda b: (b, 0, 0)),
                  pl.BlockSpec((None, n1, heads * dq), lambda b: (b, 0, 0)),
                  pl.BlockSpec((None, n1, heads * v_stride), lambda b: (b, 0, 0)),
                  pl.BlockSpec((None, tq, heads * dq), lambda b: (b, 0, 0)),
                  pl.BlockSpec((None, tq, heads * v_stride), lambda b: (b, 0, 0)),
                  _full_spec(t1.shape), _full_spec(t2.shape)],
        out_specs=pl.BlockSpec((None, tq, heads * dv), lambda b: (b, 0, 0)),
        out_shape=jax.ShapeDtypeStruct((nb, tq, heads * dv), _BF),
        compiler_params=_params("parallel"),
        name=name,
    )(q, k1, v1, k2, v2, t1, t2)


def _band_cached(qkv, cache_k, cache_v, t1, t2, layer):
    nb, tq, _ = qkv.shape
    _, _, nk, heads, d = cache_k.shape
    same = jnp.asarray(np.eye(heads, dtype=bool))
    big1 = jnp.where(same[:, None, None, :], t1[:, :, :, None], NEG).reshape(heads * tq, nk * heads)
    big2 = jnp.where(same[:, None, :, None], t2[:, :, None, :], NEG).reshape(heads * tq, heads * tq)
    rows = lambda c: c.reshape(c.shape[0], nb, nk * heads, d)
    cached = pl.BlockSpec((None, None, nk * heads, d), lambda b: (layer, b, 0, 0))
    return pl.pallas_call(
        functools.partial(_band_cached_kernel, heads=heads, d=d),
        grid=(nb,),
        in_specs=[pl.BlockSpec((None, tq, 3 * heads * d), lambda b: (b, 0, 0)), cached, cached,
                  _full_spec(big1.shape), _full_spec(big2.shape)],
        out_specs=pl.BlockSpec((None, tq, heads * d), lambda b: (b, 0, 0)),
        out_shape=jax.ShapeDtypeStruct((nb, tq, heads * d), _BF),
        compiler_params=_params("parallel"),
        name="band_attention_cached",
    )(qkv, rows(cache_k), rows(cache_v), big1, big2)


def _ret_tables(lc):
    lg = np.log1p(-np.exp2(-5.0 - np.arange(HR, dtype=np.float64)))
    idx = np.arange(lc, dtype=np.float64)
    lane_head = np.arange(LANE) // DKR
    npair = HR // 2
    dq = np.stack([np.exp((idx[:, None] + 1.0) * lg[2 * p + lane_head][None, :])
                   for p in range(npair)])
    dk = np.stack([np.exp((lc - 1.0 - idx)[:, None] * lg[2 * p + lane_head][None, :])
                   for p in range(npair)])
    diff = idx[:, None] - idx[None, :]
    dm = np.stack([np.where(diff >= 0, np.exp(np.maximum(diff, 0.0) * lg[h]), 0.0)
                   for h in range(HR)])
    rh = np.arange(2 * DKR)[:, None] // DKR
    ch = np.arange(2 * DVR)[None, :] // DVR
    ds = np.stack([np.where(rh == ch, np.exp(lc * lg[2 * p + rh]) + 0.0 * ch, 0.0)
                   for p in range(npair)])
    return [jnp.asarray(a, _F32) for a in (dq, dk, dm, ds)]


def _retention(qk, v, gr, s0, grn, nb, t, lc_pref):
    lc = _tile(t, lc_pref, CHUNK) if t > CHUNK else t
    nc = t // lc
    dq, dk, dm, ds = _ret_tables(lc)
    w = HR * DVR
    return pl.pallas_call(
        _ret_kernel,
        grid=(nb, nc),
        in_specs=[pl.BlockSpec((lc, 2 * HR * DKR), lambda b, c: (b * nc + c, 0)),
                  pl.BlockSpec((lc, w), lambda b, c: (b * nc + c, 0)),
                  pl.BlockSpec((lc, w), lambda b, c: (b * nc + c, 0)),
                  pl.BlockSpec((None, HR, DKR, DVR), lambda b, c: (b, 0, 0, 0)),
                  _full_spec(dq.shape), _full_spec(dk.shape), _full_spec(dm.shape),
                  _full_spec(ds.shape), _full_spec(grn.shape)],
        out_specs=[pl.BlockSpec((lc, w), lambda b, c: (b * nc + c, 0)),
                   pl.BlockSpec((None, HR, DKR, DVR), lambda b, c: (b, 0, 0, 0))],
        out_shape=[jax.ShapeDtypeStruct((nb * t, w), _BF),
                   jax.ShapeDtypeStruct((nb, HR, DKR, DVR), _F32)],
        scratch_shapes=[pltpu.VMEM((HR // 2, 2 * DKR, 2 * DVR), _F32)],
        compiler_params=_params("parallel", "arbitrary"),
        name="retention",
    )(qk, v, gr, s0, dq, dk, dm, ds, grn)


def _merge(h, oa, ob, oc, wg, wpa, wpb, wpc, path):
    rows, d = h.shape
    tm = path.tm(512)
    tn = _tile(d, 512, LANE)
    nj = d // tn
    wo = oa.shape[1]
    o_spec = pl.BlockSpec((tm, wo), lambda i, j: (i, 0))
    wp_spec = pl.BlockSpec((wo, tn), lambda i, j: (0, j))
    return pl.pallas_call(
        _merge_kernel,
        grid=(rows // tm, nj),
        in_specs=[pl.BlockSpec((tm, d), lambda i, j: (i, 0)), o_spec, o_spec, o_spec,
                  pl.BlockSpec((d, tn), lambda i, j: (0, j)),
                  pl.BlockSpec((d, tn), lambda i, j: (0, nj + j)),
                  pl.BlockSpec((d, tn), lambda i, j: (0, 2 * nj + j)),
                  wp_spec, wp_spec, wp_spec],
        out_specs=pl.BlockSpec((tm, tn), lambda i, j: (i, j)),
        out_shape=jax.ShapeDtypeStruct((rows, d), _BF),
        compiler_params=_params("parallel", "arbitrary"),
        name="gated_merge",
    )(h, oa, ob, oc, wg, wg, wg, wpa, wpb, wpc)


def _out_proj(mg, wo, x, gt, lng, lnb, sc, sh, path, alpha):
    rows, d = x.shape
    tm = path.tm(512)
    gt, sc, sh = path.mod(gt, tm), path.mod(sc, tm), path.mod(sh, tm)
    return pl.pallas_call(
        functools.partial(_out_kernel, alpha=alpha),
        grid=(rows // tm,),
        in_specs=[_row_spec(tm, d), _full_spec(wo.shape), _row_spec(tm, d),
                  path.mod_spec(gt, tm, 1), _full_spec(lng.shape), _full_spec(lnb.shape),
                  path.mod_spec(sc, tm, 1), path.mod_spec(sh, tm, 1)],
        out_specs=[_row_spec(tm, d), _row_spec(tm, d)],
        out_shape=[jax.ShapeDtypeStruct((rows, d), _F32), jax.ShapeDtypeStruct((rows, d), _BF)],
        compiler_params=_params("parallel"),
        name="out_proj_ln",
    )(mg, wo, x, gt, lng, lnb, sc, sh)


def _ffn(h2, x1, prev, wa, wb, cw, cb, wd, gt, lng, lnb, sc, sh, path, alpha):
    rows, d = x1.shape
    dff = wa.shape[1]
    tm = path.tm(256 if path.dense else 512)
    tf = _tile(dff, 512, LANE)
    nm, nf = rows // tm, dff // tf
    gt, sc, sh = path.mod(gt, tm), path.mod(sc, tm), path.mod(sh, tm)
    row2 = pl.BlockSpec((tm, d), lambda i, j: (i, 0))
    common_in = [row2,
                 pl.BlockSpec((d, tf), lambda i, j: (0, j)),
                 pl.BlockSpec((d, tf), lambda i, j: (0, j)),
                 pl.BlockSpec((CONV_W, tf), lambda i, j: (0, j)),
                 pl.BlockSpec((1, tf), lambda i, j: (0, j)),
                 pl.BlockSpec((tf, d), lambda i, j: (j, 0)),
                 path.mod_spec(gt, tm, 2), pl.BlockSpec((1, d), lambda i, j: (0, 0)),
                 pl.BlockSpec((1, d), lambda i, j: (0, 0)),
                 path.mod_spec(sc, tm, 2), path.mod_spec(sh, tm, 2)]
    common_args = [x1, wa, wb, cw, cb, wd, gt, lng, lnb, sc, sh]
    if path.dense:
        seg = path.t
        zeros = jnp.zeros((path.nb, seg - (CONV_W - 1), dff), _F32)
        p1 = jnp.concatenate([prev[:, 1:2], jnp.zeros((path.nb, 1, dff), _F32), zeros], 1)
        p2 = jnp.concatenate([prev, zeros], 1)
        tail = tm
        kern = functools.partial(_ffn_seg_kernel, seg=seg, alpha=alpha)
        in_specs = [row2, pl.BlockSpec((tm, tf), lambda i, j: (i, j)),
                    pl.BlockSpec((tm, tf), lambda i, j: (i, j))] + common_in
        args = [h2, p1.reshape(rows, dff), p2.reshape(rows, dff)] + common_args
        scratch = []
    else:
        hr = 16
        tail = 8
        kern = functools.partial(_ffn_halo_kernel, tiles_per_seq=path.t // tm, alpha=alpha)
        in_specs = [row2, pl.BlockSpec((hr, d), lambda i, j: (jnp.maximum(i * (tm // hr) - 1, 0), 0))
                    ] + common_in
        args = [h2, h2] + common_args
        wg = tf // FFN_GROUPS
        scratch = ([pltpu.VMEM((tm + FFN_HALO, wg), _F32)] * FFN_GROUPS
                   + [pltpu.VMEM((tm, wg), _F32)] * FFN_GROUPS + [pltpu.VMEM((tm, wg), _BF)] * FFN_GROUPS)
    y, hn, at = pl.pallas_call(
        kern,
        grid=(nm, nf),
        in_specs=in_specs,
        out_specs=[row2, row2, pl.BlockSpec((None, tail, tf), lambda i, j: (i, 0, j))],
        out_shape=[jax.ShapeDtypeStruct((rows, d), _F32), jax.ShapeDtypeStruct((rows, d), _BF),
                   jax.ShapeDtypeStruct((nm, tail, dff), _F32)],
        scratch_shapes=[pltpu.VMEM((tm, d), _F32)] + scratch,
        compiler_params=_params("parallel", "arbitrary"),
        name="conv_ffn_ln",
    )(*args)
    if path.dense:
        conv_new = at.reshape(path.nb, path.t, dff)[:, path.t - (CONV_W - 1):]
    else:
        tps = path.t // tm
        conv_new = at.reshape(path.nb, tps, tail, dff)[:, tps - 1, tail - (CONV_W - 1):]
    return y, hn, conv_new


def _layer(x, h, path, mods, nxt, cache, w, alpha):
    rows, d = x.shape
    nb, t = path.nb, path.t
    sh1, sc1, gt1, sh2, sc2, gt2 = mods
    cos, sin = path.rope_tables()
    tm_big = path.tm(1024)

    q_cat, ckv, kr128 = _latent(h, w["w_lat"], w["g_q"], w["g_kv"], w["w_uq"], cos, sin, path)
    if cache is None:
        k_t, v_a = _kvup_t(ckv, kr128, w["w_uk"].T, w["w_uv"], _mla_tiles(t)[1])
        oa = _mla_prompt(q_cat, k_t, v_a, nb, t)
    else:
        k_cat, v_a = _kvup(ckv, kr128, w["w_uk"], w["w_uv"])
        past = cache["ckv"].shape[2]
        kc, vc = _kvup(cache["ckv"].reshape(-1, KV_LORA), cache["krope"], w["w_uk"], w["w_uv"],
                       rows=nb * past, row_off=cache["layer"] * nb * past)
        q_pos = (path.pos0 + np.arange(t))[:, None] // CHUNK
        ok1 = (np.arange(past)[None, :] // CHUNK) <= q_pos
        ok2 = ((past + np.arange(t))[None, :] // CHUNK) <= q_pos
        t1 = jnp.asarray(np.where(ok1, 0.0, NEG)[None], _F32)
        t2 = jnp.asarray(np.where(ok2, 0.0, NEG)[None], _F32)
        oa = _attn2(q_cat.reshape(nb, t, -1), kc.reshape(nb, past, -1), vc.reshape(nb, past, -1),
                    k_cat.reshape(nb, t, -1), v_a.reshape(nb, t, -1), t1, t2,
                    QK_CAT, DVA, V_CAT, "mla_attention_cached")
        oa = oa.reshape(rows, HA * DVA)

    wb3 = HB * DHB
    qkv = _mm(h, w["w_band"], _BF, tm_big, wb3, "band_qkv", first_block_scale=DHB ** -0.5 * LOG2E)
    if cache is None:
        ob = _band_prompt(qkv, w["rel_bias"], nb, t)
        keep = min(BAND_PREV * CHUNK, t)
        tk = _tile(keep, 512)
        per = keep // tk
        kv_new = _mm(h, w["w_band"][:, wb3:], _F32, tk, _tile(2 * wb3, 1024, LANE), "band_kv_tail",
                     row_map=lambda i: (i // per) * (t // tk) + (t - keep) // tk + i % per,
                     out_rows=nb * keep)
        bk_new = kv_new[:, :wb3].reshape(nb, keep, HB, DHB)
        bv_new = kv_new[:, wb3:].reshape(nb, keep, HB, DHB)
    else:
        kv_new = _mm(h, w["w_band"][:, wb3:], _F32, tm_big, _tile(2 * wb3, 1024, LANE), "band_kv_tail")
        bk_new = kv_new[:, :wb3].reshape(nb, t, HB, DHB)
        bv_new = kv_new[:, wb3:].reshape(nb, t, HB, DHB)
        nk = cache["band_k"].shape[2]
        t1 = _rel_table(w["rel_bias"], nk, t, nk, (path.pos0 - nk + np.arange(nk) >= 0)[None, :])
        t2 = _rel_table(w["rel_bias"], 0, t, t, np.ones((t, t), bool))
        qkv3 = qkv.reshape(nb, t, 3 * wb3)
        ob = _band_cached(qkv3, cache["band_k"], cache["band_v"], t1, t2, cache["layer"])
        ob = ob.reshape(rows, wb3)

    tm_r = path.tm(1024)
    qk_r = pl.pallas_call(
        _retqk_kernel,
        grid=(rows // tm_r,),
        in_specs=[_row_spec(tm_r, d), _full_spec(w["w_retqk"].shape),
                  path.rope_spec(cos, tm_r, 1), path.rope_spec(sin, tm_r, 1)],
        out_specs=_row_spec(tm_r, 2 * HR * DKR),
        out_shape=jax.ShapeDtypeStruct((rows, 2 * HR * DKR), _F32),
        compiler_params=_params("parallel"),
        name="ret_qk_rope",
    )(h, w["w_retqk"], cos, sin)
    v_r = _mm(h, w["w_retv"], _BF, tm_big, HR * DVR, "ret_v")
    g_r = _mm(h, w["w_retg"], _F32, tm_big, HR * DVR, "ret_gate")
    s0 = jnp.zeros((nb, HR, DKR, DVR), _F32) if cache is None else cache["ret"]
    oc, s_new = _retention(qk_r, v_r, g_r, s0, w["g_rn"], nb, t, 256)

    mg = _merge(h, oa, ob, oc, w["w_gates"], w["w_pa"], w["w_pb"], w["w_pc"], path)
    x1, h2 = _out_proj(mg, w["w_o"], x, gt1, w["ln1_g"], w["ln1_b"], sc2, sh2, path, alpha)

    prev = None if cache is None else cache["conv"]
    x2, hn, conv_new = _ffn(h2, x1, prev, w["w_fa"], w["w_fb"], w["cw"], w["cb"], w["w_fd"],
                            gt2, w["ln2_g"], w["ln2_b"], nxt[0], nxt[1], path, alpha)
    state = (ckv.reshape(nb, t, KV_LORA), kr128[:, :DR].reshape(nb, t, DR), bk_new, bv_new,
             s_new, conv_new)
    return x2, hn, state


def _layer_weights(l, wi, w_uq, w_ukv, w_ada_unused, p):
    d = wi.shape[1]
    o = np.cumsum([0, Q_LORA, KV_LORA, DR, HB * DHB, HB * DHB, HB * DHB, HR * DKR, HR * DKR,
                   HR * DVR, HR * DVR, d, d, d]).tolist()
    cols = lambda a, b: wi[l, :, o[a]:o[b]].astype(_BF)
    lat = jnp.concatenate([cols(0, 3), jnp.zeros((d, LANE - DR), _BF)], axis=1)
    uq = w_uq[l].reshape(Q_LORA, HA, DN + DR)
    uq = jnp.pad(uq, ((0, 0), (0, 0), (0, QK_CAT - DN - DR))).reshape(Q_LORA, HA * QK_CAT)
    ukv = w_ukv[l].reshape(KV_LORA, HA, DN + DVA)
    row = lambda a: a[l][None, :]
    return {
        "w_lat": lat, "w_uq": uq,
        "w_uk": ukv[:, :, :DN].reshape(KV_LORA, HA * DN),
        "w_uv": ukv[:, :, DN:].reshape(KV_LORA, HA * DVA),
        "w_band": cols(3, 6), "w_retqk": cols(6, 8), "w_retv": cols(8, 9),
        "w_retg": cols(9, 10), "w_gates": cols(10, 13),
        "g_q": row(p["g_q_lora"]), "g_kv": row(p["g_kv_lora"]), "rel_bias": p["rel_bias"][l],
        "g_rn": row(p["g_ret_norm"]),
        "w_pa": p["w_branch_a"][l], "w_pb": p["w_branch_b"][l], "w_pc": p["w_branch_c"][l],
        "w_o": p["w_o"][l], "ln1_g": row(p["ln1_g"]), "ln1_b": row(p["ln1_b"]),
        "w_fa": p["w_ff_a"][l], "w_fb": p["w_ff_b"][l], "cw": p["conv_w"][l],
        "cb": row(p["conv_b"]), "w_fd": p["w_ff_down"][l],
        "ln2_g": row(p["ln2_g"]), "ln2_b": row(p["ln2_b"]),
    }


def kernel(x_prompt, x_sample, c_prompt, c_sample, cache_mla_ckv, cache_mla_krope, cache_band_k, cache_band_v, state_ret, state_conv, w_ada, b_ada, w_in, g_q_lora, g_kv_lora, w_uq, w_ukv, rel_bias, g_ret_norm, w_branch_a, w_branch_b, w_branch_c, w_o, ln1_g, ln1_b, w_ff_a, w_ff_b, conv_w, conv_b, w_ff_down, ln2_g, ln2_b):
    nb_p, t_p, d = x_prompt.shape
    nb_s, t_s, _ = x_sample.shape
    depth = w_in.shape[0]
    past = cache_mla_ckv.shape[2]
    alpha = (2 * depth) ** 0.25

    nc = nb_p + nb_s
    ncp = -(-nc // 16) * 16
    c_all = jnp.pad(jnp.concatenate([c_prompt, c_sample], 0), ((0, ncp - nc), (0, 0)))
    tn_a = _tile(6 * d, 2048, LANE)
    ada = pl.pallas_call(
        _ada_kernel,
        grid=(depth, 6 * d // tn_a),
        in_specs=[pl.BlockSpec((ncp, d), lambda l, j: (0, 0)),
                  pl.BlockSpec((None, d, tn_a), lambda l, j: (l, 0, j)),
                  pl.BlockSpec((None, 1, tn_a), lambda l, j: (l, 0, j))],
        out_specs=pl.BlockSpec((None, ncp, tn_a), lambda l, j: (l, 0, j)),
        out_shape=jax.ShapeDtypeStruct((depth, ncp, 6 * d), _F32),
        compiler_params=_params("parallel", "arbitrary"),
        name="ada_ln",
    )(c_all, w_ada.astype(_BF), b_ada[:, None, :])

    def mods(l, lo, n):
        a = ada[l, lo:lo + n]
        sh1, sc1, gt1, sh2, sc2, gt2 = [a[:, k * d:(k + 1) * d] for k in range(6)]
        return (sh1, 1.0 + sc1, 1.0 + gt1, sh2, 1.0 + sc2, 1.0 + gt2)

    bf = lambda a: a.astype(_BF)
    p = dict(g_q_lora=g_q_lora, g_kv_lora=g_kv_lora, rel_bias=rel_bias, g_ret_norm=g_ret_norm,
             w_branch_a=bf(w_branch_a), w_branch_b=bf(w_branch_b), w_branch_c=bf(w_branch_c),
             w_o=bf(w_o), ln1_g=ln1_g, ln1_b=ln1_b, w_ff_a=bf(w_ff_a), w_ff_b=bf(w_ff_b),
             conv_w=conv_w, conv_b=conv_b, w_ff_down=bf(w_ff_down), ln2_g=ln2_g, ln2_b=ln2_b)
    wi, wq, wkv = w_in, bf(w_uq), bf(w_ukv)
    krope_cache = jnp.pad(cache_mla_krope.reshape(-1, DR), ((0, 0), (0, LANE - DR)))

    streams = [(_Path(nb_p, t_p, 0, False), x_prompt, 0),
               (_Path(nb_s, t_s, past, True), x_sample, nb_p)]
    outs = []
    for path, x0, lo in streams:
        x = x0.reshape(path.rows, d)
        m0 = mods(0, lo, path.nb)
        h = _modulate(x, m0[1], m0[0], path)
        states = []
        for l in range(depth):
            m = mods(l, lo, path.nb)
            mn = mods(min(l + 1, depth - 1), lo, path.nb)
            cache = None
            if path.dense:
                cache = dict(layer=l, ckv=cache_mla_ckv, krope=krope_cache, band_k=cache_band_k,
                             band_v=cache_band_v, ret=state_ret[l], conv=state_conv[l])
            w = _layer_weights(l, wi, wq, wkv, None, p)
            x, h, st = _layer(x, h, path, m, (mn[1], mn[0]), cache, w, alpha)
            states.append(st)
        outs.append((x.reshape(path.nb, path.t, d), [jnp.stack(z) for z in zip(*states)]))
    (y_p, st_p), (y_s, st_s) = outs
    return (y_p, y_s, *st_p, *st_s)
```

```python
import functools

import numpy as np
import jax
import jax.numpy as jnp
from jax import lax
from jax.experimental import pallas as pl
from jax.experimental.pallas import tpu as pltpu

CHUNK = 64
HA, DN, DR, DVA = 8, 128, 64, 128
Q_LORA, KV_LORA = 512, 256
HB, DHB = 8, 128
BAND_PREV, REL_CLIP = 8, 128
HR, DKR, DVR = 8, 64, 128
CONV_W = 3
ROPE_THETA = 10000.0
EPS = 1e-5
LANE = 128
QK_CAT = DN + 2 * DR
NEG = -1e30
VMEM_LIMIT = 56 * 1024 * 1024
LOG2E = float(np.log2(np.e))
MLA_QSCALE = (DN + DR) ** -0.5 * LOG2E
V_CAT = 2 * DVA
FFN_HALO = 8
FFN_ROWS = 64
FFN_GROUPS = 2

_BF = jnp.bfloat16
_F32 = jnp.float32


def _dot(a, b):
    return jnp.dot(a, b, preferred_element_type=_F32)


def _dot_nt(a, b):
    return lax.dot_general(a, b, (((1,), (1,)), ((), ())), preferred_element_type=_F32)


def _dot_tn(a, b):
    return lax.dot_general(a, b, (((0,), (0,)), ((), ())), preferred_element_type=_F32)


def _tile(n, pref, step=8):
    if n <= pref:
        return n
    t = pref - pref % step
    while n % t:
        t -= step
    return t


def _params(*sem):
    return pltpu.CompilerParams(dimension_semantics=sem, vmem_limit_bytes=VMEM_LIMIT)


def _swap32(x):
    n = x.shape[-1]
    lane = lax.broadcasted_iota(jnp.int32, x.shape, 1)
    fwd = pltpu.roll(x, n - DR // 2, 1)
    bwd = pltpu.roll(x, DR // 2, 1)
    return jnp.where((lane % DR) < DR // 2, fwd, bwd)


def _rope128(x, cos, sin):
    return x * cos + _swap32(x) * sin


def _layer_norm(z, g, b):
    mu = jnp.mean(z, -1, keepdims=True)
    zc = z - mu
    var = jnp.mean(zc * zc, -1, keepdims=True)
    return zc * lax.rsqrt(var + EPS) * g + b


def _rms(x, g):
    return x * lax.rsqrt(jnp.mean(x * x, -1, keepdims=True) + EPS) * g


def _ada_kernel(c_ref, w_ref, b_ref, o_ref):
    c = c_ref[...]
    s = (c * jax.nn.sigmoid(c)).astype(_BF)
    o_ref[...] = _dot(s, w_ref[...].astype(_BF)) + b_ref[...]


def _mod_kernel(x_ref, sc_ref, sh_ref, o_ref):
    o_ref[...] = (x_ref[...] * sc_ref[...] + sh_ref[...]).astype(o_ref.dtype)


def _mm_kernel(x_ref, w_ref, o_ref, *, first_block_scale):
    y = _dot(x_ref[...], w_ref[...])
    if first_block_scale is not None:
        y = y * jnp.where(pl.program_id(1) == 0, first_block_scale, 1.0)
    o_ref[...] = y.astype(o_ref.dtype)


def _retqk_kernel(x_ref, w_ref, cos_ref, sin_ref, o_ref):
    y = _dot(x_ref[...], w_ref[...])
    cos, sin = cos_ref[...], sin_ref[...]
    nq = HR * DKR // LANE
    for c in range(2 * nq):
        blk = _rope128(y[:, c * LANE:(c + 1) * LANE], cos, sin)
        if c < nq:
            blk = blk * (DKR ** -0.5)
        o_ref[:, c * LANE:(c + 1) * LANE] = blk


def _lat_kernel(x_ref, wl_ref, gq_ref, gkv_ref, wuq_ref, cos_ref, sin_ref,
                q_ref, ckv_ref, kr_ref):
    y = _dot(x_ref[...], wl_ref[...])
    cos, sin = cos_ref[...], sin_ref[...]
    cqn = _rms(y[:, :Q_LORA], gq_ref[...])
    ckv_ref[...] = _rms(y[:, Q_LORA:Q_LORA + KV_LORA], gkv_ref[...])
    kr_ref[...] = _rope128(y[:, Q_LORA + KV_LORA:], cos, sin)
    qa = _dot(cqn.astype(_BF), wuq_ref[...]) * MLA_QSCALE
    for h in range(HA):
        o = h * QK_CAT
        q_ref[:, o:o + DN] = qa[:, o:o + DN].astype(_BF)
        q_ref[:, o + DN:o + QK_CAT] = _rope128(qa[:, o + DN:o + QK_CAT], cos, sin).astype(_BF)


def _kvup_kernel(ckv_ref, kr_ref, wk_ref, wv_ref, k_ref, v_ref):
    c = ckv_ref[...].astype(_BF)
    kn = _dot(c, wk_ref[...])
    vn = _dot(c, wv_ref[...])
    krb = kr_ref[...].astype(_BF)
    ones = jnp.ones((c.shape[0], DVA), _BF)
    for h in range(HA):
        o = h * QK_CAT
        k_ref[:, o:o + DN] = kn[:, h * DN:(h + 1) * DN].astype(_BF)
        k_ref[:, o + DN:o + QK_CAT] = krb
        v_ref[:, h * V_CAT:h * V_CAT + DVA] = vn[:, h * DVA:(h + 1) * DVA].astype(_BF)
        v_ref[:, h * V_CAT + DVA:(h + 1) * V_CAT] = ones


def _kvup_t_kernel(ckv_ref, kr_ref, wkt_ref, wv_ref, kt_ref, v_ref):
    c = ckv_ref[...].astype(_BF)
    knt = _dot_nt(wkt_ref[...], c)
    vn = _dot(c, wv_ref[...])
    krt = kr_ref[...].T.astype(_BF)
    ones = jnp.ones((c.shape[0], DVA), _BF)
    for h in range(HA):
        o = h * QK_CAT
        kt_ref[o:o + DN, :] = knt[h * DN:(h + 1) * DN, :].astype(_BF)
        kt_ref[o + DN:o + QK_CAT, :] = krt
        v_ref[:, h * V_CAT:h * V_CAT + DVA] = vn[:, h * DVA:(h + 1) * DVA].astype(_BF)
        v_ref[:, h * V_CAT + DVA:(h + 1) * V_CAT] = ones


def _mla_kernel(q_ref, kt_ref, v_ref, o_ref, s_scr, p_scr, m_scr, a_scr, acc_scr, *, hg, tq, tk):
    qi = pl.program_id(2)
    kc = tk // CHUNK
    m_scr[...] = jnp.full(m_scr.shape, NEG, _F32)
    acc_scr[...] = jnp.zeros(acc_scr.shape, _F32)
    col = lax.broadcasted_iota(jnp.int32, (CHUNK, tk), 1)

    def scores(kt, slot, row0=0):
        for h in range(hg):
            s_scr[slot, h, row0:, :] = _dot(q_ref[row0:, h * QK_CAT:(h + 1) * QK_CAT],
                                            kt_ref[kt, h * QK_CAT:(h + 1) * QK_CAT, :])

    def consume(kt, slot, diag):
        start = pl.multiple_of(kt * tk, tk)
        row0 = 0 if diag is None else diag * tk
        for h in range(hg):
            for r in range(row0 // CHUNK, tq // CHUNK):
                rows = slice(r * CHUNK, (r + 1) * CHUNK)
                ncol = tk if diag is None else min((r - diag * kc + 1) * CHUNK, tk)
                s = s_scr[slot, h, rows, :]
                if ncol < tk:
                    s = jnp.where(col < ncol, s, NEG)
                m_old = m_scr[h, rows, :]
                m_new = jnp.maximum(m_old, jnp.max(s, -1, keepdims=True))
                p_scr[h, rows, :] = jnp.exp2(s - m_new).astype(_BF)
                a_scr[h, rows, :] = jnp.exp2(m_old - m_new)
                m_scr[h, rows, :] = m_new
            acc_scr[h, row0:, :] = a_scr[h, row0:, :] * acc_scr[h, row0:, :] + _dot(
                p_scr[h, row0:, :], v_ref[pl.ds(start, tk), h * V_CAT:(h + 1) * V_CAT])

    def pair(j, c):
        scores(2 * j + 1, 1)
        consume(2 * j, 0, None)
        scores(2 * j + 2, 0)
        consume(2 * j + 1, 1, None)
        return c

    scores(0, 0)
    lax.fori_loop(0, qi, pair, 0)
    scores(2 * qi + 1, 1, tk)
    consume(2 * qi, 0, 0)
    consume(2 * qi + 1, 1, 1)
    for h in range(hg):
        acc = acc_scr[h]
        o_ref[:, h * DVA:(h + 1) * DVA] = (acc[:, :DVA] / acc[:, DVA:]).astype(o_ref.dtype)


def _band_kernel(q_ref, k_ref, v_ref, tb_ref, o_ref, *, hg, win, chunks_per_tile):
    n = pl.program_id(2)
    ws = pl.multiple_of(jnp.maximum(n * chunks_per_tile - BAND_PREV, 0) * CHUNK, CHUNK)
    for h in range(hg):
        q = q_ref[:, h * DHB:(h + 1) * DHB]
        k = k_ref[pl.ds(ws, win), h * DHB:(h + 1) * DHB]
        v = v_ref[pl.ds(ws, win), h * DHB:(h + 1) * DHB]
        s = _dot_nt(q, k) + tb_ref[h]
        p = jnp.exp2(s - jnp.max(s, -1, keepdims=True))
        l = jnp.sum(p, -1, keepdims=True)
        o_ref[:, h * DHB:(h + 1) * DHB] = (_dot(p.astype(_BF), v) / l).astype(o_ref.dtype)


def _attn2_kernel(q_ref, k1_ref, v1_ref, k2_ref, v2_ref, t1_ref, t2_ref, o_ref,
                  *, heads, dq, dv, v_stride, per_head_table):
    for h in range(heads):
        th = h if per_head_table else 0
        q = q_ref[:, h * dq:(h + 1) * dq]
        k1 = k1_ref[:, h * dq:(h + 1) * dq].astype(_BF)
        v1 = v1_ref[:, h * v_stride:h * v_stride + dv].astype(_BF)
        k2 = k2_ref[:, h * dq:(h + 1) * dq].astype(_BF)
        v2 = v2_ref[:, h * v_stride:h * v_stride + dv].astype(_BF)
        s1 = _dot_nt(q, k1) + t1_ref[th]
        s2 = _dot_nt(q, k2) + t2_ref[th]
        m = jnp.maximum(jnp.max(s1, -1, keepdims=True), jnp.max(s2, -1, keepdims=True))
        p1 = jnp.exp2(s1 - m)
        p2 = jnp.exp2(s2 - m)
        l = jnp.sum(p1, -1, keepdims=True) + jnp.sum(p2, -1, keepdims=True)
        o = _dot(p1.astype(_BF), v1) + _dot(p2.astype(_BF), v2)
        o_ref[:, h * dv:(h + 1) * dv] = (o / l).astype(o_ref.dtype)


def _band_cached_kernel(qkv_ref, k1_ref, v1_ref, t1_ref, t2_ref, o_ref, *, heads, d):
    tq = qkv_ref.shape[0]
    stack = lambda off: jnp.concatenate(
        [qkv_ref[:, off + h * d:off + (h + 1) * d] for h in range(heads)], axis=0)
    q, k2, v2 = stack(0), stack(heads * d), stack(2 * heads * d)
    s1 = _dot_nt(q, k1_ref[...].astype(_BF)) + t1_ref[...]
    s2 = _dot_nt(q, k2) + t2_ref[...]
    m = jnp.maximum(jnp.max(s1, -1, keepdims=True), jnp.max(s2, -1, keepdims=True))
    p1 = jnp.exp2(s1 - m)
    p2 = jnp.exp2(s2 - m)
    l = jnp.sum(p1, -1, keepdims=True) + jnp.sum(p2, -1, keepdims=True)
    o = (_dot(p1.astype(_BF), v1_ref[...].astype(_BF)) + _dot(p2.astype(_BF), v2)) / l
    for h in range(heads):
        o_ref[:, h * d:(h + 1) * d] = o[h * tq:(h + 1) * tq, :].astype(o_ref.dtype)


def _ret_kernel(qk_ref, v_ref, gr_ref, s0_ref, dq_ref, dk_ref, dm_ref, ds_ref, grn_ref,
                o_ref, sout_ref, s_scr):
    ci = pl.program_id(1)
    npair = HR // 2
    r2 = lax.broadcasted_iota(jnp.int32, (2 * DKR, 2 * DVR), 0) // DKR
    c2 = lax.broadcasted_iota(jnp.int32, (2 * DKR, 2 * DVR), 1) // DVR
    on_diag = r2 == c2

    @pl.when(ci == 0)
    def _():
        for p in range(npair):
            top = jnp.concatenate([s0_ref[2 * p], jnp.zeros((DKR, DVR), _F32)], axis=1)
            bot = jnp.concatenate([jnp.zeros((DKR, DVR), _F32), s0_ref[2 * p + 1]], axis=1)
            s_scr[p] = jnp.concatenate([top, bot], axis=0)

    lane = lax.broadcasted_iota(jnp.int32, (qk_ref.shape[0], LANE), 1)
    for p in range(npair):
        q2 = qk_ref[:, p * LANE:(p + 1) * LANE]
        k2 = qk_ref[:, HR * DKR + p * LANE:HR * DKR + (p + 1) * LANE]
        v2 = v_ref[:, 2 * p * DVR:(2 * p + 2) * DVR]
        k2b = k2.astype(_BF)
        s2 = s_scr[p]
        cross = _dot((q2 * dq_ref[p]).astype(_BF), s2.astype(_BF))
        for e in range(2):
            h = 2 * p + e
            qm = jnp.where((lane // DKR) == e, q2, 0.0).astype(_BF)
            inner = _dot_nt(qm, k2b) * dm_ref[h]
            o = _dot(inner.astype(_BF), v2[:, e * DVR:(e + 1) * DVR])
            o = o + cross[:, e * DVR:(e + 1) * DVR]
            mu = jnp.mean(o, -1, keepdims=True)
            oc = o - mu
            var = jnp.mean(oc * oc, -1, keepdims=True)
            on = oc * lax.rsqrt(var + EPS) * grn_ref[:, h * DVR:(h + 1) * DVR]
            g = gr_ref[:, h * DVR:(h + 1) * DVR]
            o_ref[:, h * DVR:(h + 1) * DVR] = (g * jax.nn.sigmoid(g) * on).astype(o_ref.dtype)
        upd = _dot_tn((k2 * dk_ref[p]).astype(_BF), v2)
        s_scr[p] = s2 * ds_ref[p] + jnp.where(on_diag, upd, 0.0)

    @pl.when(ci == pl.num_programs(1) - 1)
    def _():
        for p in range(npair):
            sp = s_scr[p]
            sout_ref[2 * p] = sp[:DKR, :DVR]
            sout_ref[2 * p + 1] = sp[DKR:, DVR:]


def _merge_kernel(h_ref, oa_ref, ob_ref, oc_ref, wga_ref, wgb_ref, wgc_ref,
                  wpa_ref, wpb_ref, wpc_ref, o_ref):
    h = h_ref[...]

    def branch(o_r, wg_r, wp_r):
        return jax.nn.sigmoid(_dot(h, wg_r[...])) * _dot(o_r[...], wp_r[...])

    m = branch(oa_ref, wga_ref, wpa_ref) + branch(ob_ref, wgb_ref, wpb_ref)
    m = m + branch(oc_ref, wgc_ref, wpc_ref)
    o_ref[...] = m.astype(o_ref.dtype)


def _out_kernel(mg_ref, wo_ref, x_ref, gt_ref, lng_ref, lnb_ref, sc_ref, sh_ref,
                x1_ref, h2_ref, *, alpha):
    y = _dot(mg_ref[...], wo_ref[...])
    x1 = _layer_norm(alpha * x_ref[...] + gt_ref[...] * y, lng_ref[...], lnb_ref[...])
    x1_ref[...] = x1
    h2_ref[...] = (x1 * sc_ref[...] + sh_ref[...]).astype(h2_ref.dtype)


def _gated(a, a1, a2, bb, cw, cb):
    conv = cb + a2 * cw[0:1, :] + a1 * cw[1:2, :] + a * cw[2:3, :]
    gelu = 0.5 * conv * (1.0 + lax.erf(conv * (2.0 ** -0.5)))
    return (gelu * bb).astype(_BF)


def _ffn_clear(acc_ref):
    @pl.when(pl.program_id(1) == 0)
    def _():
        acc_ref[...] = jnp.zeros(acc_ref.shape, _F32)


def _ffn_norm(x_ref, gt_ref, lng_ref, lnb_ref, sc_ref, sh_ref, y_ref, hn_ref, acc_ref, alpha):
    @pl.when(pl.program_id(1) == pl.num_programs(1) - 1)
    def _():
        z = alpha * x_ref[...] + gt_ref[...] * acc_ref[...]
        y = _layer_norm(z, lng_ref[...], lnb_ref[...])
        y_ref[...] = y
        hn_ref[...] = (y * sc_ref[...] + sh_ref[...]).astype(hn_ref.dtype)


def _ffn_halo_kernel(h_ref, halo_ref, x_ref, wa_ref, wb_ref, cw_ref, cb_ref, wd_ref, gt_ref,
                     lng_ref, lnb_ref, sc_ref, sh_ref, y_ref, hn_ref, at_ref, acc_ref,
                     *scr, tiles_per_seq, alpha):
    ng = len(scr) // 3
    a_scr, b_scr, g_scr = scr[:ng], scr[ng:2 * ng], scr[2 * ng:]
    i = pl.program_id(0)
    h = h_ref[...]
    halo = halo_ref[...]
    tm = h.shape[0]
    w = wa_ref.shape[1] // ng
    keep = jnp.where(i % tiles_per_seq == 0, 0.0, 1.0)
    _ffn_clear(acc_ref)
    for c in range(ng):
        cs = slice(c * w, (c + 1) * w)
        a_scr[c][FFN_HALO:, :] = _dot(h, wa_ref[:, cs])
        b_scr[c][...] = _dot(h, wb_ref[:, cs])
        ah = _dot(halo, wa_ref[:, cs])
        a_scr[c][:FFN_HALO, :] = ah[ah.shape[0] - FFN_HALO:, :] * keep
        at_ref[:, cs] = a_scr[c][tm:, :]
    rc = min(FFN_ROWS, tm)
    for c in range(ng):
        cs = slice(c * w, (c + 1) * w)
        cw, cb = cw_ref[:, cs], cb_ref[:, cs]
        for k in range(tm // rc):
            r = FFN_HALO + k * rc
            g_scr[c][r - FFN_HALO:r - FFN_HALO + rc, :] = _gated(
                a_scr[c][r:r + rc, :], a_scr[c][r - 1:r - 1 + rc, :], a_scr[c][r - 2:r - 2 + rc, :],
                b_scr[c][r - FFN_HALO:r - FFN_HALO + rc, :], cw, cb)
        acc_ref[...] += _dot(g_scr[c][...], wd_ref[cs, :])
    _ffn_norm(x_ref, gt_ref, lng_ref, lnb_ref, sc_ref, sh_ref, y_ref, hn_ref, acc_ref, alpha)


def _ffn_seg_kernel(h_ref, p1_ref, p2_ref, x_ref, wa_ref, wb_ref, cw_ref, cb_ref, wd_ref,
                    gt_ref, lng_ref, lnb_ref, sc_ref, sh_ref, y_ref, hn_ref, at_ref, acc_ref,
                    *, seg, alpha):
    h = h_ref[...]
    a = _dot(h, wa_ref[...])
    bb = _dot(h, wb_ref[...])
    rs = lax.broadcasted_iota(jnp.int32, a.shape, 0) % seg
    a1 = jnp.where(rs == 0, p1_ref[...], pltpu.roll(a, 1, 0))
    a2 = jnp.where(rs < 2, p2_ref[...], pltpu.roll(a, 2, 0))
    at_ref[...] = a
    _ffn_clear(acc_ref)
    acc_ref[...] += _dot(_gated(a, a1, a2, bb, cw_ref[...], cb_ref[...]), wd_ref[...])
    _ffn_norm(x_ref, gt_ref, lng_ref, lnb_ref, sc_ref, sh_ref, y_ref, hn_ref, acc_ref, alpha)


class _Path:
    def __init__(self, nb, t, pos0, dense):
        self.nb, self.t, self.rows, self.pos0, self.dense = nb, t, nb * t, pos0, dense

    def tm(self, pref):
        return _tile(self.rows if self.dense else self.t, pref)

    def mod(self, v, tm):
        d = v.shape[-1]
        if self.dense:
            return jnp.repeat(v, self.t, axis=0).reshape(self.rows // tm, tm, d)
        return v[:, None, :]

    def mod_spec(self, arr, tm, rank):
        g, r, d = arr.shape
        tpg = (self.rows // tm) // g
        if rank == 1:
            return pl.BlockSpec((None, r, d), lambda i: (i // tpg, 0, 0))
        return pl.BlockSpec((None, r, d), lambda i, j: (i // tpg, 0, 0))

    def rope_tables(self):
        half = DR // 2
        inv = ROPE_THETA ** (-jnp.arange(half, dtype=_F32) / half)
        pos = self.pos0 + jnp.arange(self.t, dtype=jnp.int32)
        ang = pos.astype(_F32)[:, None] * inv[None, :]
        cos, sin = jnp.cos(ang), jnp.sin(ang)
        cos = jnp.concatenate([cos, cos, cos, cos], -1)
        sin = jnp.concatenate([-sin, sin, -sin, sin], -1)
        if self.dense:
            cos, sin = jnp.tile(cos, (self.nb, 1)), jnp.tile(sin, (self.nb, 1))
        return cos, sin

    def rope_spec(self, tab, tm, rank):
        nt = tab.shape[0] // tm
        if rank == 1:
            return pl.BlockSpec((tm, LANE), lambda i: (i % nt, 0))
        return pl.BlockSpec((tm, LANE), lambda i, j: (i % nt, 0))


def _row_spec(tm, n):
    return pl.BlockSpec((tm, n), lambda i: (i, 0))


def _full_spec(shape):
    nd = len(shape)
    return pl.BlockSpec(shape, lambda *_: (0,) * nd)


def _mm(x, w, out_dtype, tm, tn, name, row_map=None, out_rows=None, first_block_scale=None):
    rows, k = x.shape
    n = w.shape[1]
    out_rows = rows if out_rows is None else out_rows
    row_map = (lambda i: i) if row_map is None else row_map
    return pl.pallas_call(
        functools.partial(_mm_kernel, first_block_scale=first_block_scale),
        grid=(out_rows // tm, n // tn),
        in_specs=[pl.BlockSpec((tm, k), lambda i, j: (row_map(i), 0)),
                  pl.BlockSpec((k, tn), lambda i, j: (0, j))],
        out_specs=pl.BlockSpec((tm, tn), lambda i, j: (i, j)),
        out_shape=jax.ShapeDtypeStruct((out_rows, n), out_dtype),
        compiler_params=_params("parallel", "arbitrary"),
        name=name,
    )(x, w)


def _modulate(x, sc, sh, path):
    rows, d = x.shape
    tm = path.tm(1024)
    sc, sh = path.mod(sc, tm), path.mod(sh, tm)
    return pl.pallas_call(
        _mod_kernel,
        grid=(rows // tm,),
        in_specs=[_row_spec(tm, d), path.mod_spec(sc, tm, 1), path.mod_spec(sh, tm, 1)],
        out_specs=_row_spec(tm, d),
        out_shape=jax.ShapeDtypeStruct((rows, d), _BF),
        compiler_params=_params("parallel"),
        name="modulate",
    )(x, sc, sh)


def _latent(h, wl, gq, gkv, wuq, cos, sin, path):
    rows, d = h.shape
    tm = path.tm(512)
    nl = wl.shape[1]
    return pl.pallas_call(
        _lat_kernel,
        grid=(rows // tm,),
        in_specs=[_row_spec(tm, d), _full_spec(wl.shape), _full_spec(gq.shape),
                  _full_spec(gkv.shape), _full_spec(wuq.shape),
                  path.rope_spec(cos, tm, 1), path.rope_spec(sin, tm, 1)],
        out_specs=[_row_spec(tm, HA * QK_CAT), _row_spec(tm, KV_LORA), _row_spec(tm, LANE)],
        out_shape=[jax.ShapeDtypeStruct((rows, HA * QK_CAT), _BF),
                   jax.ShapeDtypeStruct((rows, KV_LORA), _F32),
                   jax.ShapeDtypeStruct((rows, LANE), _F32)],
        compiler_params=_params("parallel"),
        name="mla_latent",
    )(h, wl, gq, gkv, wuq, cos, sin)


def _kvup(ckv, kr, wk, wv, rows=None, row_off=0):
    rows = ckv.shape[0] if rows is None else rows
    tm = _tile(rows, 1024)
    off = row_off // tm
    return pl.pallas_call(
        _kvup_kernel,
        grid=(rows // tm,),
        in_specs=[pl.BlockSpec((tm, KV_LORA), lambda i: (off + i, 0)),
                  pl.BlockSpec((tm, LANE), lambda i: (off + i, 0)), _full_spec(wk.shape),
                  _full_spec(wv.shape)],
        out_specs=[_row_spec(tm, HA * QK_CAT), _row_spec(tm, HA * V_CAT)],
        out_shape=[jax.ShapeDtypeStruct((rows, HA * QK_CAT), _BF),
                   jax.ShapeDtypeStruct((rows, HA * V_CAT), _BF)],
        compiler_params=_params("parallel"),
        name="mla_kv_up",
    )(ckv, kr, wk, wv)


def _kvup_t(ckv, kr, wkt, wv, tk):
    rows = ckv.shape[0]
    return pl.pallas_call(
        _kvup_t_kernel,
        grid=(rows // tk,),
        in_specs=[_row_spec(tk, KV_LORA), _row_spec(tk, LANE), _full_spec(wkt.shape),
                  _full_spec(wv.shape)],
        out_specs=[pl.BlockSpec((None, HA * QK_CAT, tk), lambda i: (i, 0, 0)),
                   _row_spec(tk, HA * V_CAT)],
        out_shape=[jax.ShapeDtypeStruct((rows // tk, HA * QK_CAT, tk), _BF),
                   jax.ShapeDtypeStruct((rows, HA * V_CAT), _BF)],
        compiler_params=_params("parallel"),
        name="mla_kv_up_t",
    )(ckv, kr, wkt, wv)


def _mla_tiles(t):
    tk = _tile(t // 2, 512, CHUNK)
    assert t % (2 * tk) == 0
    return 2 * tk, tk


def _mla_prompt(q, kt, v, nb, t):
    hg = 2
    tq, tk = _mla_tiles(t)
    nq, nk = t // tq, t // tk
    kern = functools.partial(_mla_kernel, hg=hg, tq=tq, tk=tk)
    return pl.pallas_call(
        kern,
        grid=(nb, HA // hg, nq),
        in_specs=[pl.BlockSpec((tq, hg * QK_CAT), lambda b, g, i: (b * nq + i, g)),
                  pl.BlockSpec((nk, hg * QK_CAT, tk), lambda b, g, i: (b, g, 0)),
                  pl.BlockSpec((t, hg * V_CAT), lambda b, g, i: (b, g))],
        out_specs=pl.BlockSpec((tq, hg * DVA), lambda b, g, i: (b * nq + i, g)),
        out_shape=jax.ShapeDtypeStruct((nb * t, HA * DVA), _BF),
        scratch_shapes=[pltpu.VMEM((2, hg, tq, tk), _F32), pltpu.VMEM((hg, tq, tk), _BF),
                        pltpu.VMEM((hg, tq, 1), _F32), pltpu.VMEM((hg, tq, 1), _F32),
                        pltpu.VMEM((hg, tq, V_CAT), _F32)],
        compiler_params=_params("parallel", "parallel", "arbitrary"),
        name="mla_attention",
    )(q, kt, v)


def _band_tables(rel_bias, tq, t):
    r = tq // CHUNK
    win = (r + BAND_PREV) * CHUNK
    nv = -(-BAND_PREV // r) + 1
    i = np.arange(tq)[:, None]
    m = np.arange(win)[None, :]
    tabs = []
    for var in range(nv):
        n0 = var * r
        ws = max(n0 - BAND_PREV, 0)
        cq = n0 + i // CHUNK
        ck = ws + m // CHUNK
        allowed = (ck <= cq) & (ck >= cq - BAND_PREV)
        tabs.append(_rel_table(rel_bias, (n0 - ws) * CHUNK, tq, win, allowed))
    return jnp.stack(tabs), win, nv


def _rel_table(rel_bias, off, nq, nk, allowed):
    period = nq + nk
    u = np.arange(period)
    u = np.where(u < nk, u, u - period)
    pick = np.zeros((2 * REL_CLIP + 1, period), np.float32)
    pick[np.clip(off - u, -REL_CLIP, REL_CLIP) + REL_CLIP, np.arange(period)] = 1.0
    f = jnp.dot(rel_bias, pick, precision=lax.Precision.HIGHEST) * LOG2E
    circ = jnp.tile(f, (1, nq))[:, :nq * (period - 1)].reshape(-1, nq, period - 1)[:, :, :nk]
    return jnp.where(jnp.asarray(np.broadcast_to(allowed, (nq, nk)))[None], circ, NEG).astype(_F32)


def _band_prompt(qkv, rel_bias, nb, t):
    hg = 4
    tq = _tile(t, 256, CHUNK)
    nq = t // tq
    tabs, win, nv = _band_tables(rel_bias, tq, t)
    ng = HB // hg
    kern = functools.partial(_band_kernel, hg=hg, win=win, chunks_per_tile=tq // CHUNK)
    return pl.pallas_call(
        kern,
        grid=(nb, ng, nq),
        in_specs=[pl.BlockSpec((tq, hg * DHB), lambda b, g, i: (b * nq + i, g)),
                  pl.BlockSpec((t, hg * DHB), lambda b, g, i: (b, ng + g)),
                  pl.BlockSpec((t, hg * DHB), lambda b, g, i: (b, 2 * ng + g)),
                  pl.BlockSpec((None, hg, tq, win),
                               lambda b, g, i: (jnp.minimum(i, nv - 1), g, 0, 0))],
        out_specs=pl.BlockSpec((tq, hg * DHB), lambda b, g, i: (b * nq + i, g)),
        out_shape=jax.ShapeDtypeStruct((nb * t, HB * DHB), _BF),
        compiler_params=_params("parallel", "parallel", "arbitrary"),
        name="band_attention",
    )(qkv, qkv, qkv, tabs)


def _attn2(q, k1, v1, k2, v2, t1, t2, dq, dv, v_stride, name):
    nb, tq, _ = q.shape
    heads = HA
    n1 = k1.shape[1]
    kern = functools.partial(_attn2_kernel, heads=heads, dq=dq, dv=dv, v_stride=v_stride,
                             per_head_table=t1.shape[0] > 1)
    return pl.pallas_call(
        kern,
        grid=(nb,),
        in_specs=[pl.BlockSpec((None, tq, heads * dq), lambda b: (b, 0, 0)),
                  pl.BlockSpec((None, n1, heads * dq), lambda b: (b, 0, 0)),
                  pl.BlockSpec((None, n1, heads * v_stride), lambda b: (b, 0, 0)),
                  pl.BlockSpec((None, tq, heads * dq), lambda b: (b, 0, 0)),
                  pl.BlockSpec((None, tq, heads * v_stride), lambda b: (b, 0, 0)),
                  _full_spec(t1.shape), _full_spec(t2.shape)],
        out_specs=pl.BlockSpec((None, tq, heads * dv), lambda b: (b, 0, 0)),
        out_shape=jax.ShapeDtypeStruct((nb, tq, heads * dv), _BF),
        compiler_params=_params("parallel"),
        name=name,
    )(q, k1, v1, k2, v2, t1, t2)


def _band_cached(qkv, cache_k, cache_v, t1, t2, layer):
    nb, tq, _ = qkv.shape
    _, _, nk, heads, d = cache_k.shape
    same = jnp.asarray(np.eye(heads, dtype=bool))
    big1 = jnp.where(same[:, None, None, :], t1[:, :, :, None], NEG).reshape(heads * tq, nk * heads)
    big2 = jnp.where(same[:, None, :, None], t2[:, :, None, :], NEG).reshape(heads * tq, heads * tq)
    rows = lambda c: c.reshape(c.shape[0], nb, nk * heads, d)
    cached = pl.BlockSpec((None, None, nk * heads, d), lambda b: (layer, b, 0, 0))
    return pl.pallas_call(
        functools.partial(_band_cached_kernel, heads=heads, d=d),
        grid=(nb,),
        in_specs=[pl.BlockSpec((None, tq, 3 * heads * d), lambda b: (b, 0, 0)), cached, cached,
                  _full_spec(big1.shape), _full_spec(big2.shape)],
        out_specs=pl.BlockSpec((None, tq, heads * d), lambda b: (b, 0, 0)),
        out_shape=jax.ShapeDtypeStruct((nb, tq, heads * d), _BF),
        compiler_params=_params("parallel"),
        name="band_attention_cached",
    )(qkv, rows(cache_k), rows(cache_v), big1, big2)


def _ret_tables(lc):
    lg = np.log1p(-np.exp2(-5.0 - np.arange(HR, dtype=np.float64)))
    idx = np.arange(lc, dtype=np.float64)
    lane_head = np.arange(LANE) // DKR
    npair = HR // 2
    dq = np.stack([np.exp((idx[:, None] + 1.0) * lg[2 * p + lane_head][None, :])
                   for p in range(npair)])
    dk = np.stack([np.exp((lc - 1.0 - idx)[:, None] * lg[2 * p + lane_head][None, :])
                   for p in range(npair)])
    diff = idx[:, None] - idx[None, :]
    dm = np.stack([np.where(diff >= 0, np.exp(np.maximum(diff, 0.0) * lg[h]), 0.0)
                   for h in range(HR)])
    rh = np.arange(2 * DKR)[:, None] // DKR
    ch = np.arange(2 * DVR)[None, :] // DVR
    ds = np.stack([np.where(rh == ch, np.exp(lc * lg[2 * p + rh]) + 0.0 * ch, 0.0)
                   for p in range(npair)])
    return [jnp.asarray(a, _F32) for a in (dq, dk, dm, ds)]


def _retention(qk, v, gr, s0, grn, nb, t, lc_pref):
    lc = _tile(t, lc_pref, CHUNK) if t > CHUNK else t
    nc = t // lc
    dq, dk, dm, ds = _ret_tables(lc)
    w = HR * DVR
    return pl.pallas_call(
        _ret_kernel,
        grid=(nb, nc),
        in_specs=[pl.BlockSpec((lc, 2 * HR * DKR), lambda b, c: (b * nc + c, 0)),
                  pl.BlockSpec((lc, w), lambda b, c: (b * nc + c, 0)),
                  pl.BlockSpec((lc, w), lambda b, c: (b * nc + c, 0)),
                  pl.BlockSpec((None, HR, DKR, DVR), lambda b, c: (b, 0, 0, 0)),
                  _full_spec(dq.shape), _full_spec(dk.shape), _full_spec(dm.shape),
                  _full_spec(ds.shape), _full_spec(grn.shape)],
        out_specs=[pl.BlockSpec((lc, w), lambda b, c: (b * nc + c, 0)),
                   pl.BlockSpec((None, HR, DKR, DVR), lambda b, c: (b, 0, 0, 0))],
        out_shape=[jax.ShapeDtypeStruct((nb * t, w), _BF),
                   jax.ShapeDtypeStruct((nb, HR, DKR, DVR), _F32)],
        scratch_shapes=[pltpu.VMEM((HR // 2, 2 * DKR, 2 * DVR), _F32)],
        compiler_params=_params("parallel", "arbitrary"),
        name="retention",
    )(qk, v, gr, s0, dq, dk, dm, ds, grn)


def _merge(h, oa, ob, oc, wg, wpa, wpb, wpc, path):
    rows, d = h.shape
    tm = path.tm(512)
    tn = _tile(d, 1024, LANE)
    nj = d // tn
    wo = oa.shape[1]
    o_spec = pl.BlockSpec((tm, wo), lambda i, j: (i, 0))
    wp_spec = pl.BlockSpec((wo, tn), lambda i, j: (0, j))
    return pl.pallas_call(
        _merge_kernel,
        grid=(rows // tm, nj),
        in_specs=[pl.BlockSpec((tm, d), lambda i, j: (i, 0)), o_spec, o_spec, o_spec,
                  pl.BlockSpec((d, tn), lambda i, j: (0, j)),
                  pl.BlockSpec((d, tn), lambda i, j: (0, nj + j)),
                  pl.BlockSpec((d, tn), lambda i, j: (0, 2 * nj + j)),
                  wp_spec, wp_spec, wp_spec],
        out_specs=pl.BlockSpec((tm, tn), lambda i, j: (i, j)),
        out_shape=jax.ShapeDtypeStruct((rows, d), _BF),
        compiler_params=_params("parallel", "arbitrary"),
        name="gated_merge",
    )(h, oa, ob, oc, wg, wg, wg, wpa, wpb, wpc)


def _out_proj(mg, wo, x, gt, lng, lnb, sc, sh, path, alpha):
    rows, d = x.shape
    tm = path.tm(512)
    gt, sc, sh = path.mod(gt, tm), path.mod(sc, tm), path.mod(sh, tm)
    return pl.pallas_call(
        functools.partial(_out_kernel, alpha=alpha),
        grid=(rows // tm,),
        in_specs=[_row_spec(tm, d), _full_spec(wo.shape), _row_spec(tm, d),
                  path.mod_spec(gt, tm, 1), _full_spec(lng.shape), _full_spec(lnb.shape),
                  path.mod_spec(sc, tm, 1), path.mod_spec(sh, tm, 1)],
        out_specs=[_row_spec(tm, d), _row_spec(tm, d)],
        out_shape=[jax.ShapeDtypeStruct((rows, d), _F32), jax.ShapeDtypeStruct((rows, d), _BF)],
        compiler_params=_params("parallel"),
        name="out_proj_ln",
    )(mg, wo, x, gt, lng, lnb, sc, sh)


def _ffn(h2, x1, prev, layer, wa, wb, cw, cb, wd, gt, lng, lnb, sc, sh, path, alpha):
    rows, d = x1.shape
    dff = wa.shape[2]
    tm = path.tm(256 if path.dense else 512)
    tf = _tile(dff, 512, LANE)
    nm, nf = rows // tm, dff // tf
    gt, sc, sh = path.mod(gt, tm), path.mod(sc, tm), path.mod(sh, tm)
    row2 = pl.BlockSpec((tm, d), lambda i, j: (i, 0))
    common_in = [row2,
                 pl.BlockSpec((None, d, tf), lambda i, j: (layer, 0, j)),
                 pl.BlockSpec((None, d, tf), lambda i, j: (layer, 0, j)),
                 pl.BlockSpec((CONV_W, tf), lambda i, j: (0, j)),
                 pl.BlockSpec((1, tf), lambda i, j: (0, j)),
                 pl.BlockSpec((None, tf, d), lambda i, j: (layer, j, 0)),
                 path.mod_spec(gt, tm, 2), pl.BlockSpec((1, d), lambda i, j: (0, 0)),
                 pl.BlockSpec((1, d), lambda i, j: (0, 0)),
                 path.mod_spec(sc, tm, 2), path.mod_spec(sh, tm, 2)]
    common_args = [x1, wa, wb, cw, cb, wd, gt, lng, lnb, sc, sh]
    if path.dense:
        seg = path.t
        zeros = jnp.zeros((path.nb, seg - (CONV_W - 1), dff), _F32)
        p1 = jnp.concatenate([prev[:, 1:2], jnp.zeros((path.nb, 1, dff), _F32), zeros], 1)
        p2 = jnp.concatenate([prev, zeros], 1)
        tail = tm
        kern = functools.partial(_ffn_seg_kernel, seg=seg, alpha=alpha)
        in_specs = [row2, pl.BlockSpec((tm, tf), lambda i, j: (i, j)),
                    pl.BlockSpec((tm, tf), lambda i, j: (i, j))] + common_in
        args = [h2, p1.reshape(rows, dff), p2.reshape(rows, dff)] + common_args
        scratch = []
    else:
        hr = 16
        tail = 8
        kern = functools.partial(_ffn_halo_kernel, tiles_per_seq=path.t // tm, alpha=alpha)
        in_specs = [row2, pl.BlockSpec((hr, d), lambda i, j: (jnp.maximum(i * (tm // hr) - 1, 0), 0))
                    ] + common_in
        args = [h2, h2] + common_args
        wg = tf // FFN_GROUPS
        scratch = ([pltpu.VMEM((tm + FFN_HALO, wg), _F32)] * FFN_GROUPS
                   + [pltpu.VMEM((tm, wg), _F32)] * FFN_GROUPS + [pltpu.VMEM((tm, wg), _BF)] * FFN_GROUPS)
    y, hn, at = pl.pallas_call(
        kern,
        grid=(nm, nf),
        in_specs=in_specs,
        out_specs=[row2, row2, pl.BlockSpec((None, tail, tf), lambda i, j: (i, 0, j))],
        out_shape=[jax.ShapeDtypeStruct((rows, d), _F32), jax.ShapeDtypeStruct((rows, d), _BF),
                   jax.ShapeDtypeStruct((nm, tail, dff), _F32)],
        scratch_shapes=[pltpu.VMEM((tm, d), _F32)] + scratch,
        compiler_params=_params("parallel", "arbitrary"),
        name="conv_ffn_ln",
    )(*args)
    if path.dense:
        conv_new = at.reshape(path.nb, path.t, dff)[:, path.t - (CONV_W - 1):]
    else:
        tps = path.t // tm
        conv_new = at.reshape(path.nb, tps, tail, dff)[:, tps - 1, tail - (CONV_W - 1):]
    return y, hn, conv_new


def _layer(x, h, path, mods, nxt, cache, w, alpha):
    rows, d = x.shape
    nb, t = path.nb, path.t
    sh1, sc1, gt1, sh2, sc2, gt2 = mods
    cos, sin = path.rope_tables()
    tm_big = path.tm(1024)

    q_cat, ckv, kr128 = _latent(h, w["w_lat"], w["g_q"], w["g_kv"], w["w_uq"], cos, sin, path)
    if cache is None:
        k_t, v_a = _kvup_t(ckv, kr128, w["w_uk"].T, w["w_uv"], _mla_tiles(t)[1])
        oa = _mla_prompt(q_cat, k_t, v_a, nb, t)
    else:
        k_cat, v_a = _kvup(ckv, kr128, w["w_uk"], w["w_uv"])
        past = cache["ckv"].shape[2]
        kc, vc = _kvup(cache["ckv"].reshape(-1, KV_LORA), cache["krope"], w["w_uk"], w["w_uv"],
                       rows=nb * past, row_off=cache["layer"] * nb * past)
        q_pos = (path.pos0 + np.arange(t))[:, None] // CHUNK
        ok1 = (np.arange(past)[None, :] // CHUNK) <= q_pos
        ok2 = ((past + np.arange(t))[None, :] // CHUNK) <= q_pos
        t1 = jnp.asarray(np.where(ok1, 0.0, NEG)[None], _F32)
        t2 = jnp.asarray(np.where(ok2, 0.0, NEG)[None], _F32)
        oa = _attn2(q_cat.reshape(nb, t, -1), kc.reshape(nb, past, -1), vc.reshape(nb, past, -1),
                    k_cat.reshape(nb, t, -1), v_a.reshape(nb, t, -1), t1, t2,
                    QK_CAT, DVA, V_CAT, "mla_attention_cached")
        oa = oa.reshape(rows, HA * DVA)

    wb3 = HB * DHB
    qkv = _mm(h, w["w_band"], _BF, tm_big, wb3, "band_qkv", first_block_scale=DHB ** -0.5 * LOG2E)
    if cache is None:
        ob = _band_prompt(qkv, w["rel_bias"], nb, t)
        keep = min(BAND_PREV * CHUNK, t)
        tk = _tile(keep, 512)
        per = keep // tk
        kv_new = _mm(h, w["w_band"][:, wb3:], _F32, tk, _tile(2 * wb3, 1024, LANE), "band_kv_tail",
                     row_map=lambda i: (i // per) * (t // tk) + (t - keep) // tk + i % per,
                     out_rows=nb * keep)
        bk_new = kv_new[:, :wb3].reshape(nb, keep, HB, DHB)
        bv_new = kv_new[:, wb3:].reshape(nb, keep, HB, DHB)
    else:
        kv_new = _mm(h, w["w_band"][:, wb3:], _F32, tm_big, _tile(2 * wb3, 1024, LANE), "band_kv_tail")
        bk_new = kv_new[:, :wb3].reshape(nb, t, HB, DHB)
        bv_new = kv_new[:, wb3:].reshape(nb, t, HB, DHB)
        nk = cache["band_k"].shape[2]
        t1 = _rel_table(w["rel_bias"], nk, t, nk, (path.pos0 - nk + np.arange(nk) >= 0)[None, :])
        t2 = _rel_table(w["rel_bias"], 0, t, t, np.ones((t, t), bool))
        qkv3 = qkv.reshape(nb, t, 3 * wb3)
        ob = _band_cached(qkv3, cache["band_k"], cache["band_v"], t1, t2, cache["layer"])
        ob = ob.reshape(rows, wb3)

    tm_r = path.tm(1024)
    qk_r = pl.pallas_call(
        _retqk_kernel,
        grid=(rows // tm_r,),
        in_specs=[_row_spec(tm_r, d), _full_spec(w["w_retqk"].shape),
                  path.rope_spec(cos, tm_r, 1), path.rope_spec(sin, tm_r, 1)],
        out_specs=_row_spec(tm_r, 2 * HR * DKR),
        out_shape=jax.ShapeDtypeStruct((rows, 2 * HR * DKR), _F32),
        compiler_params=_params("parallel"),
        name="ret_qk_rope",
    )(h, w["w_retqk"], cos, sin)
    v_r = _mm(h, w["w_retv"], _BF, tm_big, HR * DVR, "ret_v")
    g_r = _mm(h, w["w_retg"], _F32, tm_big, HR * DVR, "ret_gate")
    s0 = jnp.zeros((nb, HR, DKR, DVR), _F32) if cache is None else cache["ret"]
    oc, s_new = _retention(qk_r, v_r, g_r, s0, w["g_rn"], nb, t, 256)

    mg = _merge(h, oa, ob, oc, w["w_gates"], w["w_pa"], w["w_pb"], w["w_pc"], path)
    x1, h2 = _out_proj(mg, w["w_o"], x, gt1, w["ln1_g"], w["ln1_b"], sc2, sh2, path, alpha)

    prev = None if cache is None else cache["conv"]
    x2, hn, conv_new = _ffn(h2, x1, prev, w["layer"], w["w_fa"], w["w_fb"], w["cw"], w["cb"], w["w_fd"],
                            gt2, w["ln2_g"], w["ln2_b"], nxt[0], nxt[1], path, alpha)
    state = (ckv.reshape(nb, t, KV_LORA), kr128[:, :DR].reshape(nb, t, DR), bk_new, bv_new,
             s_new, conv_new)
    return x2, hn, state


def _layer_weights(l, wi, w_uq, w_ukv, w_ada_unused, p):
    d = wi.shape[1]
    o = np.cumsum([0, Q_LORA, KV_LORA, DR, HB * DHB, HB * DHB, HB * DHB, HR * DKR, HR * DKR,
                   HR * DVR, HR * DVR, d, d, d]).tolist()
    cols = lambda a, b: wi[l, :, o[a]:o[b]].astype(_BF)
    lat = jnp.concatenate([cols(0, 3), jnp.zeros((d, LANE - DR), _BF)], axis=1)
    uq = w_uq[l].reshape(Q_LORA, HA, DN + DR)
    uq = jnp.pad(uq, ((0, 0), (0, 0), (0, QK_CAT - DN - DR))).reshape(Q_LORA, HA * QK_CAT)
    ukv = w_ukv[l].reshape(KV_LORA, HA, DN + DVA)
    row = lambda a: a[l][None, :]
    return {
        "w_lat": lat, "w_uq": uq,
        "w_uk": ukv[:, :, :DN].reshape(KV_LORA, HA * DN),
        "w_uv": ukv[:, :, DN:].reshape(KV_LORA, HA * DVA),
        "w_band": cols(3, 6), "w_retqk": cols(6, 8), "w_retv": cols(8, 9),
        "w_retg": cols(9, 10), "w_gates": cols(10, 13),
        "g_q": row(p["g_q_lora"]), "g_kv": row(p["g_kv_lora"]), "rel_bias": p["rel_bias"][l],
        "g_rn": row(p["g_ret_norm"]),
        "w_pa": p["w_branch_a"][l], "w_pb": p["w_branch_b"][l], "w_pc": p["w_branch_c"][l],
        "w_o": p["w_o"][l], "ln1_g": row(p["ln1_g"]), "ln1_b": row(p["ln1_b"]),
        "layer": l, "w_fa": p["w_ff_a"], "w_fb": p["w_ff_b"], "cw": p["conv_w"][l],
        "cb": row(p["conv_b"]), "w_fd": p["w_ff_down"],
        "ln2_g": row(p["ln2_g"]), "ln2_b": row(p["ln2_b"]),
    }


def kernel(x_prompt, x_sample, c_prompt, c_sample, cache_mla_ckv, cache_mla_krope, cache_band_k, cache_band_v, state_ret, state_conv, w_ada, b_ada, w_in, g_q_lora, g_kv_lora, w_uq, w_ukv, rel_bias, g_ret_norm, w_branch_a, w_branch_b, w_branch_c, w_o, ln1_g, ln1_b, w_ff_a, w_ff_b, conv_w, conv_b, w_ff_down, ln2_g, ln2_b):
    nb_p, t_p, d = x_prompt.shape
    nb_s, t_s, _ = x_sample.shape
    depth = w_in.shape[0]
    past = cache_mla_ckv.shape[2]
    alpha = (2 * depth) ** 0.25

    nc = nb_p + nb_s
    ncp = -(-nc // 16) * 16
    c_all = jnp.pad(jnp.concatenate([c_prompt, c_sample], 0), ((0, ncp - nc), (0, 0)))
    tn_a = _tile(6 * d, 1024, LANE)
    ada = pl.pallas_call(
        _ada_kernel,
        grid=(depth, 6 * d // tn_a),
        in_specs=[pl.BlockSpec((ncp, d), lambda l, j: (0, 0)),
                  pl.BlockSpec((None, d, tn_a), lambda l, j: (l, 0, j)),
                  pl.BlockSpec((None, 1, tn_a), lambda l, j: (l, 0, j))],
        out_specs=pl.BlockSpec((None, ncp, tn_a), lambda l, j: (l, 0, j)),
        out_shape=jax.ShapeDtypeStruct((depth, ncp, 6 * d), _F32),
        compiler_params=_params("parallel", "arbitrary"),
        name="ada_ln",
    )(c_all, w_ada, b_ada[:, None, :])

    def mods(l, lo, n):
        a = ada[l, lo:lo + n]
        sh1, sc1, gt1, sh2, sc2, gt2 = [a[:, k * d:(k + 1) * d] for k in range(6)]
        return (sh1, 1.0 + sc1, 1.0 + gt1, sh2, 1.0 + sc2, 1.0 + gt2)

    bf = lambda a: a.astype(_BF)
    p = dict(g_q_lora=g_q_lora, g_kv_lora=g_kv_lora, rel_bias=rel_bias, g_ret_norm=g_ret_norm,
             w_branch_a=bf(w_branch_a), w_branch_b=bf(w_branch_b), w_branch_c=bf(w_branch_c),
             w_o=bf(w_o), ln1_g=ln1_g, ln1_b=ln1_b, w_ff_a=bf(w_ff_a), w_ff_b=bf(w_ff_b),
             conv_w=conv_w, conv_b=conv_b, w_ff_down=bf(w_ff_down), ln2_g=ln2_g, ln2_b=ln2_b)
    wi, wq, wkv = w_in, bf(w_uq), bf(w_ukv)
    krope_cache = jnp.pad(cache_mla_krope.reshape(-1, DR), ((0, 0), (0, LANE - DR)))

    streams = [(_Path(nb_p, t_p, 0, False), x_prompt, 0),
               (_Path(nb_s, t_s, past, True), x_sample, nb_p)]
    outs = []
    for path, x0, lo in streams:
        x = x0.reshape(path.rows, d)
        m0 = mods(0, lo, path.nb)
        h = _modulate(x, m0[1], m0[0], path)
        states = []
        for l in range(depth):
            m = mods(l, lo, path.nb)
            mn = mods(min(l + 1, depth - 1), lo, path.nb)
            cache = None
            if path.dense:
                cache = dict(layer=l, ckv=cache_mla_ckv, krope=krope_cache, band_k=cache_band_k,
                             band_v=cache_band_v, ret=state_ret[l], conv=state_conv[l])
            w = _layer_weights(l, wi, wq, wkv, None, p)
            x, h, st = _layer(x, h, path, m, (mn[1], mn[0]), cache, w, alpha)
            states.append(st)
        outs.append((x.reshape(path.nb, path.t, d), [jnp.stack(z) for z in zip(*states)]))
    (y_p, st_p), (y_s, st_s) = outs
    return (y_p, y_s, *st_p, *st_s)
```

```python
import functools

import numpy as np
import jax
import jax.numpy as jnp
from jax import lax
from jax.experimental import pallas as pl
from jax.experimental.pallas import tpu as pltpu

CHUNK = 64
HA, DN, DR, DVA = 8, 128, 64, 128
Q_LORA, KV_LORA = 512, 256
HB, DHB = 8, 128
BAND_PREV, REL_CLIP = 8, 128
HR, DKR, DVR = 8, 64, 128
CONV_W = 3
ROPE_THETA = 10000.0
EPS = 1e-5
LANE = 128
QK_CAT = DN + 2 * DR
NEG = -1e30
VMEM_LIMIT = 56 * 1024 * 1024
LOG2E = float(np.log2(np.e))
MLA_QSCALE = (DN + DR) ** -0.5 * LOG2E
V_CAT = 2 * DVA
FFN_HALO = 8
FFN_ROWS = 64
FFN_GROUPS = 2

_BF = jnp.bfloat16
_F32 = jnp.float32


def _dot(a, b):
    return jnp.dot(a, b, preferred_element_type=_F32)


def _dot_nt(a, b):
    return lax.dot_general(a, b, (((1,), (1,)), ((), ())), preferred_element_type=_F32)


def _dot_tn(a, b):
    return lax.dot_general(a, b, (((0,), (0,)), ((), ())), preferred_element_type=_F32)


def _tile(n, pref, step=8):
    if n <= pref:
        return n
    t = pref - pref % step
    while n % t:
        t -= step
    return t


def _params(*sem):
    return pltpu.CompilerParams(dimension_semantics=sem, vmem_limit_bytes=VMEM_LIMIT)


def _swap32(x):
    n = x.shape[-1]
    lane = lax.broadcasted_iota(jnp.int32, x.shape, 1)
    fwd = pltpu.roll(x, n - DR // 2, 1)
    bwd = pltpu.roll(x, DR // 2, 1)
    return jnp.where((lane % DR) < DR // 2, fwd, bwd)


def _rope128(x, cos, sin):
    return x * cos + _swap32(x) * sin


def _layer_norm(z, g, b):
    mu = jnp.mean(z, -1, keepdims=True)
    zc = z - mu
    var = jnp.mean(zc * zc, -1, keepdims=True)
    return zc * lax.rsqrt(var + EPS) * g + b


def _rms(x, g):
    return x * lax.rsqrt(jnp.mean(x * x, -1, keepdims=True) + EPS) * g


def _ada_kernel(c_ref, w_ref, b_ref, o_ref):
    c = c_ref[...]
    s = (c * jax.nn.sigmoid(c)).astype(_BF)
    o_ref[...] = _dot(s, w_ref[...].astype(_BF)) + b_ref[...]


def _mod_kernel(x_ref, sc_ref, sh_ref, o_ref):
    o_ref[...] = (x_ref[...] * sc_ref[...] + sh_ref[...]).astype(o_ref.dtype)


def _mm_kernel(x_ref, w_ref, o_ref, *, first_block_scale):
    y = _dot(x_ref[...], w_ref[...])
    if first_block_scale is not None:
        y = y * jnp.where(pl.program_id(1) == 0, first_block_scale, 1.0)
    o_ref[...] = y.astype(o_ref.dtype)


def _retqk_kernel(x_ref, w_ref, cos_ref, sin_ref, o_ref):
    y = _dot(x_ref[...], w_ref[...])
    cos, sin = cos_ref[...], sin_ref[...]
    nq = HR * DKR // LANE
    for c in range(2 * nq):
        blk = _rope128(y[:, c * LANE:(c + 1) * LANE], cos, sin)
        if c < nq:
            blk = blk * (DKR ** -0.5)
        o_ref[:, c * LANE:(c + 1) * LANE] = blk


def _lat_kernel(x_ref, wl_ref, gq_ref, gkv_ref, wuq_ref, cos_ref, sin_ref,
                q_ref, ckv_ref, kr_ref):
    y = _dot(x_ref[...], wl_ref[...])
    cos, sin = cos_ref[...], sin_ref[...]
    cqn = _rms(y[:, :Q_LORA], gq_ref[...])
    ckv_ref[...] = _rms(y[:, Q_LORA:Q_LORA + KV_LORA], gkv_ref[...])
    kr_ref[...] = _rope128(y[:, Q_LORA + KV_LORA:], cos, sin)
    qa = _dot(cqn.astype(_BF), wuq_ref[...]) * MLA_QSCALE
    for h in range(HA):
        o = h * QK_CAT
        q_ref[:, o:o + DN] = qa[:, o:o + DN].astype(_BF)
        q_ref[:, o + DN:o + QK_CAT] = _rope128(qa[:, o + DN:o + QK_CAT], cos, sin).astype(_BF)


def _kvup_kernel(ckv_ref, kr_ref, wk_ref, wv_ref, k_ref, v_ref):
    c = ckv_ref[...].astype(_BF)
    kn = _dot(c, wk_ref[...])
    vn = _dot(c, wv_ref[...])
    krb = kr_ref[...].astype(_BF)
    ones = jnp.ones((c.shape[0], DVA), _BF)
    for h in range(HA):
        o = h * QK_CAT
        k_ref[:, o:o + DN] = kn[:, h * DN:(h + 1) * DN].astype(_BF)
        k_ref[:, o + DN:o + QK_CAT] = krb
        v_ref[:, h * V_CAT:h * V_CAT + DVA] = vn[:, h * DVA:(h + 1) * DVA].astype(_BF)
        v_ref[:, h * V_CAT + DVA:(h + 1) * V_CAT] = ones


def _kvup_t_kernel(ckv_ref, kr_ref, wkt_ref, wv_ref, kt_ref, v_ref):
    c = ckv_ref[...].astype(_BF)
    knt = _dot_nt(wkt_ref[...], c)
    vn = _dot(c, wv_ref[...])
    krt = kr_ref[...].T.astype(_BF)
    ones = jnp.ones((c.shape[0], DVA), _BF)
    for h in range(HA):
        o = h * QK_CAT
        kt_ref[o:o + DN, :] = knt[h * DN:(h + 1) * DN, :].astype(_BF)
        kt_ref[o + DN:o + QK_CAT, :] = krt
        v_ref[:, h * V_CAT:h * V_CAT + DVA] = vn[:, h * DVA:(h + 1) * DVA].astype(_BF)
        v_ref[:, h * V_CAT + DVA:(h + 1) * V_CAT] = ones


def _mla_kernel(q_ref, kt_ref, v_ref, o_ref, s_scr, p_scr, m_scr, a_scr, acc_scr, *, hg, tq, tk):
    qi = pl.program_id(2)
    kc = tk // CHUNK
    m_scr[...] = jnp.full(m_scr.shape, NEG, _F32)
    acc_scr[...] = jnp.zeros(acc_scr.shape, _F32)
    col = lax.broadcasted_iota(jnp.int32, (CHUNK, tk), 1)

    def scores(kt, slot, row0=0):
        for h in range(hg):
            s_scr[slot, h, row0:, :] = _dot(q_ref[row0:, h * QK_CAT:(h + 1) * QK_CAT],
                                            kt_ref[kt, h * QK_CAT:(h + 1) * QK_CAT, :])

    def consume(kt, slot, diag):
        start = pl.multiple_of(kt * tk, tk)
        row0 = 0 if diag is None else diag * tk
        for h in range(hg):
            for r in range(row0 // CHUNK, tq // CHUNK):
                rows = slice(r * CHUNK, (r + 1) * CHUNK)
                ncol = tk if diag is None else min((r - diag * kc + 1) * CHUNK, tk)
                s = s_scr[slot, h, rows, :]
                if ncol < tk:
                    s = jnp.where(col < ncol, s, NEG)
                m_old = m_scr[h, rows, :]
                m_new = jnp.maximum(m_old, jnp.max(s, -1, keepdims=True))
                p_scr[h, rows, :] = jnp.exp2(s - m_new).astype(_BF)
                a_scr[h, rows, :] = jnp.exp2(m_old - m_new)
                m_scr[h, rows, :] = m_new
            acc_scr[h, row0:, :] = a_scr[h, row0:, :] * acc_scr[h, row0:, :] + _dot(
                p_scr[h, row0:, :], v_ref[pl.ds(start, tk), h * V_CAT:(h + 1) * V_CAT])

    def pair(j, c):
        scores(2 * j + 1, 1)
        consume(2 * j, 0, None)
        scores(2 * j + 2, 0)
        consume(2 * j + 1, 1, None)
        return c

    scores(0, 0)
    lax.fori_loop(0, qi, pair, 0)
    scores(2 * qi + 1, 1, tk)
    consume(2 * qi, 0, 0)
    consume(2 * qi + 1, 1, 1)
    for h in range(hg):
        acc = acc_scr[h]
        o_ref[:, h * DVA:(h + 1) * DVA] = (acc[:, :DVA] / acc[:, DVA:]).astype(o_ref.dtype)


def _band_kernel(q_ref, k_ref, v_ref, tb_ref, o_ref, *, hg, win, chunks_per_tile):
    n = pl.program_id(2)
    ws = pl.multiple_of(jnp.maximum(n * chunks_per_tile - BAND_PREV, 0) * CHUNK, CHUNK)
    for h in range(hg):
        q = q_ref[:, h * DHB:(h + 1) * DHB]
        k = k_ref[pl.ds(ws, win), h * DHB:(h + 1) * DHB]
        v = v_ref[pl.ds(ws, win), h * DHB:(h + 1) * DHB]
        s = _dot_nt(q, k) + tb_ref[h]
        p = jnp.exp2(s - jnp.max(s, -1, keepdims=True))
        l = jnp.sum(p, -1, keepdims=True)
        o_ref[:, h * DHB:(h + 1) * DHB] = (_dot(p.astype(_BF), v) / l).astype(o_ref.dtype)


def _attn2_kernel(q_ref, k1_ref, v1_ref, k2_ref, v2_ref, t1_ref, t2_ref, o_ref,
                  *, heads, dq, dv, v_stride, per_head_table):
    for h in range(heads):
        th = h if per_head_table else 0
        q = q_ref[:, h * dq:(h + 1) * dq]
        k1 = k1_ref[:, h * dq:(h + 1) * dq].astype(_BF)
        v1 = v1_ref[:, h * v_stride:h * v_stride + dv].astype(_BF)
        k2 = k2_ref[:, h * dq:(h + 1) * dq].astype(_BF)
        v2 = v2_ref[:, h * v_stride:h * v_stride + dv].astype(_BF)
        s1 = _dot_nt(q, k1) + t1_ref[th]
        s2 = _dot_nt(q, k2) + t2_ref[th]
        m = jnp.maximum(jnp.max(s1, -1, keepdims=True), jnp.max(s2, -1, keepdims=True))
        p1 = jnp.exp2(s1 - m)
        p2 = jnp.exp2(s2 - m)
        l = jnp.sum(p1, -1, keepdims=True) + jnp.sum(p2, -1, keepdims=True)
        o = _dot(p1.astype(_BF), v1) + _dot(p2.astype(_BF), v2)
        o_ref[:, h * dv:(h + 1) * dv] = (o / l).astype(o_ref.dtype)


def _band_cached_kernel(qkv_ref, k1_ref, v1_ref, t1_ref, t2_ref, o_ref, *, heads, d):
    tq = qkv_ref.shape[0]
    stack = lambda off: jnp.concatenate(
        [qkv_ref[:, off + h * d:off + (h + 1) * d] for h in range(heads)], axis=0)
    q, k2, v2 = stack(0), stack(heads * d), stack(2 * heads * d)
    s1 = _dot_nt(q, k1_ref[...].astype(_BF)) + t1_ref[...]
    s2 = _dot_nt(q, k2) + t2_ref[...]
    m = jnp.maximum(jnp.max(s1, -1, keepdims=True), jnp.max(s2, -1, keepdims=True))
    p1 = jnp.exp2(s1 - m)
    p2 = jnp.exp2(s2 - m)
    l = jnp.sum(p1, -1, keepdims=True) + jnp.sum(p2, -1, keepdims=True)
    o = (_dot(p1.astype(_BF), v1_ref[...].astype(_BF)) + _dot(p2.astype(_BF), v2)) / l
    for h in range(heads):
        o_ref[:, h * d:(h + 1) * d] = o[h * tq:(h + 1) * tq, :].astype(o_ref.dtype)


def _ret_kernel(qk_ref, v_ref, gr_ref, s0_ref, dq_ref, dk_ref, dm_ref, ds_ref, grn_ref,
                o_ref, sout_ref, s_scr):
    ci = pl.program_id(1)
    npair = HR // 2
    r2 = lax.broadcasted_iota(jnp.int32, (2 * DKR, 2 * DVR), 0) // DKR
    c2 = lax.broadcasted_iota(jnp.int32, (2 * DKR, 2 * DVR), 1) // DVR
    on_diag = r2 == c2

    @pl.when(ci == 0)
    def _():
        for p in range(npair):
            top = jnp.concatenate([s0_ref[2 * p], jnp.zeros((DKR, DVR), _F32)], axis=1)
            bot = jnp.concatenate([jnp.zeros((DKR, DVR), _F32), s0_ref[2 * p + 1]], axis=1)
            s_scr[p] = jnp.concatenate([top, bot], axis=0)

    lane = lax.broadcasted_iota(jnp.int32, (qk_ref.shape[0], LANE), 1)
    for p in range(npair):
        q2 = qk_ref[:, p * LANE:(p + 1) * LANE]
        k2 = qk_ref[:, HR * DKR + p * LANE:HR * DKR + (p + 1) * LANE]
        v2 = v_ref[:, 2 * p * DVR:(2 * p + 2) * DVR]
        k2b = k2.astype(_BF)
        s2 = s_scr[p]
        cross = _dot((q2 * dq_ref[p]).astype(_BF), s2.astype(_BF))
        for e in range(2):
            h = 2 * p + e
            qm = jnp.where((lane // DKR) == e, q2, 0.0).astype(_BF)
            inner = _dot_nt(qm, k2b) * dm_ref[h]
            o = _dot(inner.astype(_BF), v2[:, e * DVR:(e + 1) * DVR])
            o = o + cross[:, e * DVR:(e + 1) * DVR]
            mu = jnp.mean(o, -1, keepdims=True)
            oc = o - mu
            var = jnp.mean(oc * oc, -1, keepdims=True)
            on = oc * lax.rsqrt(var + EPS) * grn_ref[:, h * DVR:(h + 1) * DVR]
            g = gr_ref[:, h * DVR:(h + 1) * DVR]
            o_ref[:, h * DVR:(h + 1) * DVR] = (g * jax.nn.sigmoid(g) * on).astype(o_ref.dtype)
        upd = _dot_tn((k2 * dk_ref[p]).astype(_BF), v2)
        s_scr[p] = s2 * ds_ref[p] + jnp.where(on_diag, upd, 0.0)

    @pl.when(ci == pl.num_programs(1) - 1)
    def _():
        for p in range(npair):
            sp = s_scr[p]
            sout_ref[2 * p] = sp[:DKR, :DVR]
            sout_ref[2 * p + 1] = sp[DKR:, DVR:]


def _merge_kernel(h_ref, oa_ref, ob_ref, oc_ref, wga_ref, wgb_ref, wgc_ref,
                  wpa_ref, wpb_ref, wpc_ref, o_ref):
    h = h_ref[...]

    def branch(o_r, wg_r, wp_r):
        return jax.nn.sigmoid(_dot(h, wg_r[...])) * _dot(o_r[...], wp_r[...])

    m = branch(oa_ref, wga_ref, wpa_ref) + branch(ob_ref, wgb_ref, wpb_ref)
    m = m + branch(oc_ref, wgc_ref, wpc_ref)
    o_ref[...] = m.astype(o_ref.dtype)


def _out_kernel(mg_ref, wo_ref, x_ref, gt_ref, lng_ref, lnb_ref, sc_ref, sh_ref,
                x1_ref, h2_ref, *, alpha):
    y = _dot(mg_ref[...], wo_ref[...])
    x1 = _layer_norm(alpha * x_ref[...] + gt_ref[...] * y, lng_ref[...], lnb_ref[...])
    x1_ref[...] = x1
    h2_ref[...] = (x1 * sc_ref[...] + sh_ref[...]).astype(h2_ref.dtype)


def _gated(a, a1, a2, bb, cw, cb):
    conv = cb + a2 * cw[0:1, :] + a1 * cw[1:2, :] + a * cw[2:3, :]
    gelu = 0.5 * conv * (1.0 + lax.erf(conv * (2.0 ** -0.5)))
    return (gelu * bb).astype(_BF)


def _ffn_clear(acc_ref):
    @pl.when(pl.program_id(1) == 0)
    def _():
        acc_ref[...] = jnp.zeros(acc_ref.shape, _F32)


def _ffn_norm(x_ref, gt_ref, lng_ref, lnb_ref, sc_ref, sh_ref, y_ref, hn_ref, acc_ref, alpha):
    @pl.when(pl.program_id(1) == pl.num_programs(1) - 1)
    def _():
        z = alpha * x_ref[...] + gt_ref[...] * acc_ref[...]
        y = _layer_norm(z, lng_ref[...], lnb_ref[...])
        y_ref[...] = y
        hn_ref[...] = (y * sc_ref[...] + sh_ref[...]).astype(hn_ref.dtype)


def _ffn_halo_kernel(h_ref, halo_ref, x_ref, wa_ref, wb_ref, cw_ref, cb_ref, wd_ref, gt_ref,
                     lng_ref, lnb_ref, sc_ref, sh_ref, y_ref, hn_ref, at_ref, acc_ref,
                     *scr, tiles_per_seq, alpha):
    ng = len(scr) // 3
    a_scr, b_scr, g_scr = scr[:ng], scr[ng:2 * ng], scr[2 * ng:]
    i = pl.program_id(0)
    h = h_ref[...]
    halo = halo_ref[...]
    tm = h.shape[0]
    w = wa_ref.shape[1] // ng
    keep = jnp.where(i % tiles_per_seq == 0, 0.0, 1.0)
    _ffn_clear(acc_ref)
    for c in range(ng):
        cs = slice(c * w, (c + 1) * w)
        a_scr[c][FFN_HALO:, :] = _dot(h, wa_ref[:, cs])
        b_scr[c][...] = _dot(h, wb_ref[:, cs])
        ah = _dot(halo, wa_ref[:, cs])
        a_scr[c][:FFN_HALO, :] = ah[ah.shape[0] - FFN_HALO:, :] * keep
        at_ref[:, cs] = a_scr[c][tm:, :]
    rc = min(FFN_ROWS, tm)
    for c in range(ng):
        cs = slice(c * w, (c + 1) * w)
        cw, cb = cw_ref[:, cs], cb_ref[:, cs]
        for k in range(tm // rc):
            r = FFN_HALO + k * rc
            g_scr[c][r - FFN_HALO:r - FFN_HALO + rc, :] = _gated(
                a_scr[c][r:r + rc, :], a_scr[c][r - 1:r - 1 + rc, :], a_scr[c][r - 2:r - 2 + rc, :],
                b_scr[c][r - FFN_HALO:r - FFN_HALO + rc, :], cw, cb)
        acc_ref[...] += _dot(g_scr[c][...], wd_ref[cs, :])
    _ffn_norm(x_ref, gt_ref, lng_ref, lnb_ref, sc_ref, sh_ref, y_ref, hn_ref, acc_ref, alpha)


def _ffn_seg_kernel(h_ref, p1_ref, p2_ref, x_ref, wa_ref, wb_ref, cw_ref, cb_ref, wd_ref,
                    gt_ref, lng_ref, lnb_ref, sc_ref, sh_ref, y_ref, hn_ref, at_ref, acc_ref,
                    *, seg, alpha):
    h = h_ref[...]
    a = _dot(h, wa_ref[...])
    bb = _dot(h, wb_ref[...])
    rs = lax.broadcasted_iota(jnp.int32, a.shape, 0) % seg
    a1 = jnp.where(rs == 0, p1_ref[...], pltpu.roll(a, 1, 0))
    a2 = jnp.where(rs < 2, p2_ref[...], pltpu.roll(a, 2, 0))
    at_ref[...] = a
    _ffn_clear(acc_ref)
    acc_ref[...] += _dot(_gated(a, a1, a2, bb, cw_ref[...], cb_ref[...]), wd_ref[...])
    _ffn_norm(x_ref, gt_ref, lng_ref, lnb_ref, sc_ref, sh_ref, y_ref, hn_ref, acc_ref, alpha)


class _Path:
    def __init__(self, nb, t, pos0, dense):
        self.nb, self.t, self.rows, self.pos0, self.dense = nb, t, nb * t, pos0, dense

    def tm(self, pref):
        return _tile(self.rows if self.dense else self.t, pref)

    def mod(self, v, tm):
        d = v.shape[-1]
        if self.dense:
            return jnp.repeat(v, self.t, axis=0).reshape(self.rows // tm, tm, d)
        return v[:, None, :]

    def mod_spec(self, arr, tm, rank):
        g, r, d = arr.shape
        tpg = (self.rows // tm) // g
        if rank == 1:
            return pl.BlockSpec((None, r, d), lambda i: (i // tpg, 0, 0))
        return pl.BlockSpec((None, r, d), lambda i, j: (i // tpg, 0, 0))

    def rope_tables(self):
        half = DR // 2
        inv = ROPE_THETA ** (-jnp.arange(half, dtype=_F32) / half)
        pos = self.pos0 + jnp.arange(self.t, dtype=jnp.int32)
        ang = pos.astype(_F32)[:, None] * inv[None, :]
        cos, sin = jnp.cos(ang), jnp.sin(ang)
        cos = jnp.concatenate([cos, cos, cos, cos], -1)
        sin = jnp.concatenate([-sin, sin, -sin, sin], -1)
        if self.dense:
            cos, sin = jnp.tile(cos, (self.nb, 1)), jnp.tile(sin, (self.nb, 1))
        return cos, sin

    def rope_spec(self, tab, tm, rank):
        nt = tab.shape[0] // tm
        if rank == 1:
            return pl.BlockSpec((tm, LANE), lambda i: (i % nt, 0))
        return pl.BlockSpec((tm, LANE), lambda i, j: (i % nt, 0))


def _row_spec(tm, n):
    return pl.BlockSpec((tm, n), lambda i: (i, 0))


def _full_spec(shape):
    nd = len(shape)
    return pl.BlockSpec(shape, lambda *_: (0,) * nd)


def _mm(x, w, out_dtype, tm, tn, name, row_map=None, out_rows=None, first_block_scale=None):
    rows, k = x.shape
    n = w.shape[1]
    out_rows = rows if out_rows is None else out_rows
    row_map = (lambda i: i) if row_map is None else row_map
    return pl.pallas_call(
        functools.partial(_mm_kernel, first_block_scale=first_block_scale),
        grid=(out_rows // tm, n // tn),
        in_specs=[pl.BlockSpec((tm, k), lambda i, j: (row_map(i), 0)),
                  pl.BlockSpec((k, tn), lambda i, j: (0, j))],
        out_specs=pl.BlockSpec((tm, tn), lambda i, j: (i, j)),
        out_shape=jax.ShapeDtypeStruct((out_rows, n), out_dtype),
        compiler_params=_params("parallel", "arbitrary"),
        name=name,
    )(x, w)


def _modulate(x, sc, sh, path):
    rows, d = x.shape
    tm = path.tm(1024)
    sc, sh = path.mod(sc, tm), path.mod(sh, tm)
    return pl.pallas_call(
        _mod_kernel,
        grid=(rows // tm,),
        in_specs=[_row_spec(tm, d), path.mod_spec(sc, tm, 1), path.mod_spec(sh, tm, 1)],
        out_specs=_row_spec(tm, d),
        out_shape=jax.ShapeDtypeStruct((rows, d), _BF),
        compiler_params=_params("parallel"),
        name="modulate",
    )(x, sc, sh)


def _latent(h, wl, gq, gkv, wuq, cos, sin, path):
    rows, d = h.shape
    tm = path.tm(512)
    nl = wl.shape[1]
    return pl.pallas_call(
        _lat_kernel,
        grid=(rows // tm,),
        in_specs=[_row_spec(tm, d), _full_spec(wl.shape), _full_spec(gq.shape),
                  _full_spec(gkv.shape), _full_spec(wuq.shape),
                  path.rope_spec(cos, tm, 1), path.rope_spec(sin, tm, 1)],
        out_specs=[_row_spec(tm, HA * QK_CAT), _row_spec(tm, KV_LORA), _row_spec(tm, LANE)],
        out_shape=[jax.ShapeDtypeStruct((rows, HA * QK_CAT), _BF),
                   jax.ShapeDtypeStruct((rows, KV_LORA), _F32),
                   jax.ShapeDtypeStruct((rows, LANE), _F32)],
        compiler_params=_params("parallel"),
        name="mla_latent",
    )(h, wl, gq, gkv, wuq, cos, sin)


def _kvup(ckv, kr, wk, wv, rows=None, row_off=0):
    rows = ckv.shape[0] if rows is None else rows
    tm = _tile(rows, 1024)
    off = row_off // tm
    return pl.pallas_call(
        _kvup_kernel,
        grid=(rows // tm,),
        in_specs=[pl.BlockSpec((tm, KV_LORA), lambda i: (off + i, 0)),
                  pl.BlockSpec((tm, LANE), lambda i: (off + i, 0)), _full_spec(wk.shape),
                  _full_spec(wv.shape)],
        out_specs=[_row_spec(tm, HA * QK_CAT), _row_spec(tm, HA * V_CAT)],
        out_shape=[jax.ShapeDtypeStruct((rows, HA * QK_CAT), _BF),
                   jax.ShapeDtypeStruct((rows, HA * V_CAT), _BF)],
        compiler_params=_params("parallel"),
        name="mla_kv_up",
    )(ckv, kr, wk, wv)


def _kvup_t(ckv, kr, wkt, wv, tk):
    rows = ckv.shape[0]
    return pl.pallas_call(
        _kvup_t_kernel,
        grid=(rows // tk,),
        in_specs=[_row_spec(tk, KV_LORA), _row_spec(tk, LANE), _full_spec(wkt.shape),
                  _full_spec(wv.shape)],
        out_specs=[pl.BlockSpec((None, HA * QK_CAT, tk), lambda i: (i, 0, 0)),
                   _row_spec(tk, HA * V_CAT)],
        out_shape=[jax.ShapeDtypeStruct((rows // tk, HA * QK_CAT, tk), _BF),
                   jax.ShapeDtypeStruct((rows, HA * V_CAT), _BF)],
        compiler_params=_params("parallel"),
        name="mla_kv_up_t",
    )(ckv, kr, wkt, wv)


def _mla_tiles(t):
    tk = _tile(t // 2, 512, CHUNK)
    assert t % (2 * tk) == 0
    return 2 * tk, tk


def _mla_prompt(q, kt, v, nb, t):
    hg = 2
    tq, tk = _mla_tiles(t)
    nq, nk = t // tq, t // tk
    kern = functools.partial(_mla_kernel, hg=hg, tq=tq, tk=tk)
    return pl.pallas_call(
        kern,
        grid=(nb, HA // hg, nq),
        in_specs=[pl.BlockSpec((tq, hg * QK_CAT), lambda b, g, i: (b * nq + i, g)),
                  pl.BlockSpec((nk, hg * QK_CAT, tk), lambda b, g, i: (b, g, 0)),
                  pl.BlockSpec((t, hg * V_CAT), lambda b, g, i: (b, g))],
        out_specs=pl.BlockSpec((tq, hg * DVA), lambda b, g, i: (b * nq + i, g)),
        out_shape=jax.ShapeDtypeStruct((nb * t, HA * DVA), _BF),
        scratch_shapes=[pltpu.VMEM((2, hg, tq, tk), _F32), pltpu.VMEM((hg, tq, tk), _BF),
                        pltpu.VMEM((hg, tq, 1), _F32), pltpu.VMEM((hg, tq, 1), _F32),
                        pltpu.VMEM((hg, tq, V_CAT), _F32)],
        compiler_params=_params("parallel", "parallel", "arbitrary"),
        name="mla_attention",
    )(q, kt, v)


def _band_tables(rel_bias, tq, t):
    r = tq // CHUNK
    win = (r + BAND_PREV) * CHUNK
    nv = -(-BAND_PREV // r) + 1
    i = np.arange(tq)[:, None]
    m = np.arange(win)[None, :]
    tabs = []
    for var in range(nv):
        n0 = var * r
        ws = max(n0 - BAND_PREV, 0)
        cq = n0 + i // CHUNK
        ck = ws + m // CHUNK
        allowed = (ck <= cq) & (ck >= cq - BAND_PREV)
        tabs.append(_rel_table(rel_bias, (n0 - ws) * CHUNK, tq, win, allowed))
    return jnp.stack(tabs), win, nv


def _rel_table(rel_bias, off, nq, nk, allowed):
    period = nq + nk
    u = np.arange(period)
    u = np.where(u < nk, u, u - period)
    pick = np.zeros((2 * REL_CLIP + 1, period), np.float32)
    pick[np.clip(off - u, -REL_CLIP, REL_CLIP) + REL_CLIP, np.arange(period)] = 1.0
    f = jnp.dot(rel_bias, pick, precision=lax.Precision.HIGHEST) * LOG2E
    circ = jnp.tile(f, (1, nq))[:, :nq * (period - 1)].reshape(-1, nq, period - 1)[:, :, :nk]
    return jnp.where(jnp.asarray(np.broadcast_to(allowed, (nq, nk)))[None], circ, NEG).astype(_F32)


def _band_prompt(qkv, rel_bias, nb, t):
    hg = 4
    tq = _tile(t, 256, CHUNK)
    nq = t // tq
    tabs, win, nv = _band_tables(rel_bias, tq, t)
    ng = HB // hg
    kern = functools.partial(_band_kernel, hg=hg, win=win, chunks_per_tile=tq // CHUNK)
    return pl.pallas_call(
        kern,
        grid=(nb, ng, nq),
        in_specs=[pl.BlockSpec((tq, hg * DHB), lambda b, g, i: (b * nq + i, g)),
                  pl.BlockSpec((t, hg * DHB), lambda b, g, i: (b, ng + g)),
                  pl.BlockSpec((t, hg * DHB), lambda b, g, i: (b, 2 * ng + g)),
                  pl.BlockSpec((None, hg, tq, win),
                               lambda b, g, i: (jnp.minimum(i, nv - 1), g, 0, 0))],
        out_specs=pl.BlockSpec((tq, hg * DHB), lambda b, g, i: (b * nq + i, g)),
        out_shape=jax.ShapeDtypeStruct((nb * t, HB * DHB), _BF),
        compiler_params=_params("parallel", "parallel", "arbitrary"),
        name="band_attention",
    )(qkv, qkv, qkv, tabs)


def _attn2(q, k1, v1, k2, v2, t1, t2, dq, dv, v_stride, name):
    nb, tq, _ = q.shape
    heads = HA
    n1 = k1.shape[1]
    kern = functools.partial(_attn2_kernel, heads=heads, dq=dq, dv=dv, v_stride=v_stride,
                             per_head_table=t1.shape[0] > 1)
    return pl.pallas_call(
        kern,
        grid=(nb,),
        in_specs=[pl.BlockSpec((None, tq, heads * dq), lambda b: (b, 0, 0)),
                  pl.BlockSpec((None, n1, heads * dq), lambda b: (b, 0, 0)),
                  pl.BlockSpec((None, n1, heads * v_stride), lambda b: (b, 0, 0)),
                  pl.BlockSpec((None, tq, heads * dq), lambda b: (b, 0, 0)),
                  pl.BlockSpec((None, tq, heads * v_stride), lambda b: (b, 0, 0)),
                  _full_spec(t1.shape), _full_spec(t2.shape)],
        out_specs=pl.BlockSpec((None, tq, heads * dv), lambda b: (b, 0, 0)),
        out_shape=jax.ShapeDtypeStruct((nb, tq, heads * dv), _BF),
        compiler_params=_params("parallel"),
        name=name,
    )(q, k1, v1, k2, v2, t1, t2)


def _band_cached(qkv, cache_k, cache_v, t1, t2, layer):
    nb, tq, _ = qkv.shape
    _, _, nk, heads, d = cache_k.shape
    same = jnp.asarray(np.eye(heads, dtype=bool))
    big1 = jnp.where(same[:, None, None, :], t1[:, :, :, None], NEG).reshape(heads * tq, nk * heads)
    big2 = jnp.where(same[:, None, :, None], t2[:, :, None, :], NEG).reshape(heads * tq, heads * tq)
    rows = lambda c: c.reshape(c.shape[0], nb, nk * heads, d)
    cached = pl.BlockSpec((None, None, nk * heads, d), lambda b: (layer, b, 0, 0))
    return pl.pallas_call(
        functools.partial(_band_cached_kernel, heads=heads, d=d),
        grid=(nb,),
        in_specs=[pl.BlockSpec((None, tq, 3 * heads * d), lambda b: (b, 0, 0)), cached, cached,
                  _full_spec(big1.shape), _full_spec(big2.shape)],
        out_specs=pl.BlockSpec((None, tq, heads * d), lambda b: (b, 0, 0)),
        out_shape=jax.ShapeDtypeStruct((nb, tq, heads * d), _BF),
        compiler_params=_params("parallel"),
        name="band_attention_cached",
    )(qkv, rows(cache_k), rows(cache_v), big1, big2)


def _ret_tables(lc):
    lg = np.log1p(-np.exp2(-5.0 - np.arange(HR, dtype=np.float64)))
    idx = np.arange(lc, dtype=np.float64)
    lane_head = np.arange(LANE) // DKR
    npair = HR // 2
    dq = np.stack([np.exp((idx[:, None] + 1.0) * lg[2 * p + lane_head][None, :])
                   for p in range(npair)])
    dk = np.stack([np.exp((lc - 1.0 - idx)[:, None] * lg[2 * p + lane_head][None, :])
                   for p in range(npair)])
    diff = idx[:, None] - idx[None, :]
    dm = np.stack([np.where(diff >= 0, np.exp(np.maximum(diff, 0.0) * lg[h]), 0.0)
                   for h in range(HR)])
    rh = np.arange(2 * DKR)[:, None] // DKR
    ch = np.arange(2 * DVR)[None, :] // DVR
    ds = np.stack([np.where(rh == ch, np.exp(lc * lg[2 * p + rh]) + 0.0 * ch, 0.0)
                   for p in range(npair)])
    return [jnp.asarray(a, _F32) for a in (dq, dk, dm, ds)]


def _retention(qk, v, gr, s0, grn, nb, t, lc_pref):
    lc = _tile(t, lc_pref, CHUNK) if t > CHUNK else t
    nc = t // lc
    dq, dk, dm, ds = _ret_tables(lc)
    w = HR * DVR
    return pl.pallas_call(
        _ret_kernel,
        grid=(nb, nc),
        in_specs=[pl.BlockSpec((lc, 2 * HR * DKR), lambda b, c: (b * nc + c, 0)),
                  pl.BlockSpec((lc, w), lambda b, c: (b * nc + c, 0)),
                  pl.BlockSpec((lc, w), lambda b, c: (b * nc + c, 0)),
                  pl.BlockSpec((None, HR, DKR, DVR), lambda b, c: (b, 0, 0, 0)),
                  _full_spec(dq.shape), _full_spec(dk.shape), _full_spec(dm.shape),
                  _full_spec(ds.shape), _full_spec(grn.shape)],
        out_specs=[pl.BlockSpec((lc, w), lambda b, c: (b * nc + c, 0)),
                   pl.BlockSpec((None, HR, DKR, DVR), lambda b, c: (b, 0, 0, 0))],
        out_shape=[jax.ShapeDtypeStruct((nb * t, w), _BF),
                   jax.ShapeDtypeStruct((nb, HR, DKR, DVR), _F32)],
        scratch_shapes=[pltpu.VMEM((HR // 2, 2 * DKR, 2 * DVR), _F32)],
        compiler_params=_params("parallel", "arbitrary"),
        name="retention",
    )(qk, v, gr, s0, dq, dk, dm, ds, grn)


def _merge(h, oa, ob, oc, wg, wpa, wpb, wpc, path):
    rows, d = h.shape
    tm = path.tm(512)
    tn = _tile(d, 1024, LANE)
    nj = d // tn
    wo = oa.shape[1]
    o_spec = pl.BlockSpec((tm, wo), lambda i, j: (i, 0))
    wp_spec = pl.BlockSpec((wo, tn), lambda i, j: (0, j))
    return pl.pallas_call(
        _merge_kernel,
        grid=(rows // tm, nj),
        in_specs=[pl.BlockSpec((tm, d), lambda i, j: (i, 0)), o_spec, o_spec, o_spec,
                  pl.BlockSpec((d, tn), lambda i, j: (0, j)),
                  pl.BlockSpec((d, tn), lambda i, j: (0, nj + j)),
                  pl.BlockSpec((d, tn), lambda i, j: (0, 2 * nj + j)),
                  wp_spec, wp_spec, wp_spec],
        out_specs=pl.BlockSpec((tm, tn), lambda i, j: (i, j)),
        out_shape=jax.ShapeDtypeStruct((rows, d), _BF),
        compiler_params=_params("parallel", "arbitrary"),
        name="gated_merge",
    )(h, oa, ob, oc, wg, wg, wg, wpa, wpb, wpc)


def _out_proj(mg, wo, x, gt, lng, lnb, sc, sh, path, alpha):
    rows, d = x.shape
    tm = path.tm(512)
    gt, sc, sh = path.mod(gt, tm), path.mod(sc, tm), path.mod(sh, tm)
    return pl.pallas_call(
        functools.partial(_out_kernel, alpha=alpha),
        grid=(rows // tm,),
        in_specs=[_row_spec(tm, d), _full_spec(wo.shape), _row_spec(tm, d),
                  path.mod_spec(gt, tm, 1), _full_spec(lng.shape), _full_spec(lnb.shape),
                  path.mod_spec(sc, tm, 1), path.mod_spec(sh, tm, 1)],
        out_specs=[_row_spec(tm, d), _row_spec(tm, d)],
        out_shape=[jax.ShapeDtypeStruct((rows, d), _F32), jax.ShapeDtypeStruct((rows, d), _BF)],
        compiler_params=_params("parallel"),
        name="out_proj_ln",
    )(mg, wo, x, gt, lng, lnb, sc, sh)


def _ffn(h2, x1, prev, layer, wa, wb, cw, cb, wd, gt, lng, lnb, sc, sh, path, alpha):
    rows, d = x1.shape
    dff = wa.shape[2]
    tm = path.tm(256 if path.dense else 512)
    tf = _tile(dff, 512, LANE)
    nm, nf = rows // tm, dff // tf
    gt, sc, sh = path.mod(gt, tm), path.mod(sc, tm), path.mod(sh, tm)
    row2 = pl.BlockSpec((tm, d), lambda i, j: (i, 0))
    common_in = [row2,
                 pl.BlockSpec((None, d, tf), lambda i, j: (layer, 0, j)),
                 pl.BlockSpec((None, d, tf), lambda i, j: (layer, 0, j)),
                 pl.BlockSpec((CONV_W, tf), lambda i, j: (0, j)),
                 pl.BlockSpec((1, tf), lambda i, j: (0, j)),
                 pl.BlockSpec((None, tf, d), lambda i, j: (layer, j, 0)),
                 path.mod_spec(gt, tm, 2), pl.BlockSpec((1, d), lambda i, j: (0, 0)),
                 pl.BlockSpec((1, d), lambda i, j: (0, 0)),
                 path.mod_spec(sc, tm, 2), path.mod_spec(sh, tm, 2)]
    common_args = [x1, wa, wb, cw, cb, wd, gt, lng, lnb, sc, sh]
    if path.dense:
        seg = path.t
        zeros = jnp.zeros((path.nb, seg - (CONV_W - 1), dff), _F32)
        p1 = jnp.concatenate([prev[:, 1:2], jnp.zeros((path.nb, 1, dff), _F32), zeros], 1)
        p2 = jnp.concatenate([prev, zeros], 1)
        tail = tm
        kern = functools.partial(_ffn_seg_kernel, seg=seg, alpha=alpha)
        in_specs = [row2, pl.BlockSpec((tm, tf), lambda i, j: (i, j)),
                    pl.BlockSpec((tm, tf), lambda i, j: (i, j))] + common_in
        args = [h2, p1.reshape(rows, dff), p2.reshape(rows, dff)] + common_args
        scratch = []
    else:
        hr = 16
        tail = 8
        kern = functools.partial(_ffn_halo_kernel, tiles_per_seq=path.t // tm, alpha=alpha)
        in_specs = [row2, pl.BlockSpec((hr, d), lambda i, j: (jnp.maximum(i * (tm // hr) - 1, 0), 0))
                    ] + common_in
        args = [h2, h2] + common_args
        wg = tf // FFN_GROUPS
        scratch = ([pltpu.VMEM((tm + FFN_HALO, wg), _F32)] * FFN_GROUPS
                   + [pltpu.VMEM((tm, wg), _F32)] * FFN_GROUPS + [pltpu.VMEM((tm, wg), _BF)] * FFN_GROUPS)
    y, hn, at = pl.pallas_call(
        kern,
        grid=(nm, nf),
        in_specs=in_specs,
        out_specs=[row2, row2, pl.BlockSpec((None, tail, tf), lambda i, j: (i, 0, j))],
        out_shape=[jax.ShapeDtypeStruct((rows, d), _F32), jax.ShapeDtypeStruct((rows, d), _BF),
                   jax.ShapeDtypeStruct((nm, tail, dff), _F32)],
        scratch_shapes=[pltpu.VMEM((tm, d), _F32)] + scratch,
        compiler_params=_params("parallel", "arbitrary"),
        name="conv_ffn_ln",
    )(*args)
    if path.dense:
        conv_new = at.reshape(path.nb, path.t, dff)[:, path.t - (CONV_W - 1):]
    else:
        tps = path.t // tm
        conv_new = at.reshape(path.nb, tps, tail, dff)[:, tps - 1, tail - (CONV_W - 1):]
    return y, hn, conv_new


def _layer(x, h, path, mods, nxt, cache, w, alpha):
    rows, d = x.shape
    nb, t = path.nb, path.t
    sh1, sc1, gt1, sh2, sc2, gt2 = mods
    cos, sin = path.rope_tables()
    tm_big = path.tm(2048)

    q_cat, ckv, kr128 = _latent(h, w["w_lat"], w["g_q"], w["g_kv"], w["w_uq"], cos, sin, path)
    if cache is None:
        k_t, v_a = _kvup_t(ckv, kr128, w["w_uk"].T, w["w_uv"], _mla_tiles(t)[1])
        oa = _mla_prompt(q_cat, k_t, v_a, nb, t)
    else:
        k_cat, v_a = _kvup(ckv, kr128, w["w_uk"], w["w_uv"])
        past = cache["ckv"].shape[2]
        kc, vc = _kvup(cache["ckv"].reshape(-1, KV_LORA), cache["krope"], w["w_uk"], w["w_uv"],
                       rows=nb * past, row_off=cache["layer"] * nb * past)
        q_pos = (path.pos0 + np.arange(t))[:, None] // CHUNK
        ok1 = (np.arange(past)[None, :] // CHUNK) <= q_pos
        ok2 = ((past + np.arange(t))[None, :] // CHUNK) <= q_pos
        t1 = jnp.asarray(np.where(ok1, 0.0, NEG)[None], _F32)
        t2 = jnp.asarray(np.where(ok2, 0.0, NEG)[None], _F32)
        oa = _attn2(q_cat.reshape(nb, t, -1), kc.reshape(nb, past, -1), vc.reshape(nb, past, -1),
                    k_cat.reshape(nb, t, -1), v_a.reshape(nb, t, -1), t1, t2,
                    QK_CAT, DVA, V_CAT, "mla_attention_cached")
        oa = oa.reshape(rows, HA * DVA)

    wb3 = HB * DHB
    qkv = _mm(h, w["w_band"], _BF, tm_big, wb3, "band_qkv", first_block_scale=DHB ** -0.5 * LOG2E)
    if cache is None:
        ob = _band_prompt(qkv, w["rel_bias"], nb, t)
        keep = min(BAND_PREV * CHUNK, t)
        tk = _tile(keep, 512)
        per = keep // tk
        kv_new = _mm(h, w["w_band"][:, wb3:], _F32, tk, _tile(2 * wb3, 1024, LANE), "band_kv_tail",
                     row_map=lambda i: (i // per) * (t // tk) + (t - keep) // tk + i % per,
                     out_rows=nb * keep)
        bk_new = kv_new[:, :wb3].reshape(nb, keep, HB, DHB)
        bv_new = kv_new[:, wb3:].reshape(nb, keep, HB, DHB)
    else:
        kv_new = _mm(h, w["w_band"][:, wb3:], _F32, tm_big, _tile(2 * wb3, 1024, LANE), "band_kv_tail")
        bk_new = kv_new[:, :wb3].reshape(nb, t, HB, DHB)
        bv_new = kv_new[:, wb3:].reshape(nb, t, HB, DHB)
        nk = cache["band_k"].shape[2]
        t1 = _rel_table(w["rel_bias"], nk, t, nk, (path.pos0 - nk + np.arange(nk) >= 0)[None, :])
        t2 = _rel_table(w["rel_bias"], 0, t, t, np.ones((t, t), bool))
        qkv3 = qkv.reshape(nb, t, 3 * wb3)
        ob = _band_cached(qkv3, cache["band_k"], cache["band_v"], t1, t2, cache["layer"])
        ob = ob.reshape(rows, wb3)

    tm_r = path.tm(1024)
    qk_r = pl.pallas_call(
        _retqk_kernel,
        grid=(rows // tm_r,),
        in_specs=[_row_spec(tm_r, d), _full_spec(w["w_retqk"].shape),
                  path.rope_spec(cos, tm_r, 1), path.rope_spec(sin, tm_r, 1)],
        out_specs=_row_spec(tm_r, 2 * HR * DKR),
        out_shape=jax.ShapeDtypeStruct((rows, 2 * HR * DKR), _F32),
        compiler_params=_params("parallel"),
        name="ret_qk_rope",
    )(h, w["w_retqk"], cos, sin)
    v_r = _mm(h, w["w_retv"], _BF, tm_big, HR * DVR, "ret_v")
    g_r = _mm(h, w["w_retg"], _F32, tm_big, HR * DVR, "ret_gate")
    s0 = jnp.zeros((nb, HR, DKR, DVR), _F32) if cache is None else cache["ret"]
    oc, s_new = _retention(qk_r, v_r, g_r, s0, w["g_rn"], nb, t, 256)

    mg = _merge(h, oa, ob, oc, w["w_gates"], w["w_pa"], w["w_pb"], w["w_pc"], path)
    x1, h2 = _out_proj(mg, w["w_o"], x, gt1, w["ln1_g"], w["ln1_b"], sc2, sh2, path, alpha)

    prev = None if cache is None else cache["conv"]
    x2, hn, conv_new = _ffn(h2, x1, prev, w["layer"], w["w_fa"], w["w_fb"], w["cw"], w["cb"], w["w_fd"],
                            gt2, w["ln2_g"], w["ln2_b"], nxt[0], nxt[1], path, alpha)
    state = (ckv.reshape(nb, t, KV_LORA), kr128[:, :DR].reshape(nb, t, DR), bk_new, bv_new,
             s_new, conv_new)
    return x2, hn, state


def _layer_weights(l, wi, w_uq, w_ukv, w_ada_unused, p):
    d = wi.shape[1]
    o = np.cumsum([0, Q_LORA, KV_LORA, DR, HB * DHB, HB * DHB, HB * DHB, HR * DKR, HR * DKR,
                   HR * DVR, HR * DVR, d, d, d]).tolist()
    cols = lambda a, b: wi[l, :, o[a]:o[b]].astype(_BF)
    lat = jnp.concatenate([cols(0, 3), jnp.zeros((d, LANE - DR), _BF)], axis=1)
    uq = w_uq[l].reshape(Q_LORA, HA, DN + DR)
    uq = jnp.pad(uq, ((0, 0), (0, 0), (0, QK_CAT - DN - DR))).reshape(Q_LORA, HA * QK_CAT)
    ukv = w_ukv[l].reshape(KV_LORA, HA, DN + DVA)
    row = lambda a: a[l][None, :]
    return {
        "w_lat": lat, "w_uq": uq,
        "w_uk": ukv[:, :, :DN].reshape(KV_LORA, HA * DN),
        "w_uv": ukv[:, :, DN:].reshape(KV_LORA, HA * DVA),
        "w_band": cols(3, 6), "w_retqk": cols(6, 8), "w_retv": cols(8, 9),
        "w_retg": cols(9, 10), "w_gates": cols(10, 13),
        "g_q": row(p["g_q_lora"]), "g_kv": row(p["g_kv_lora"]), "rel_bias": p["rel_bias"][l],
        "g_rn": row(p["g_ret_norm"]),
        "w_pa": p["w_branch_a"][l], "w_pb": p["w_branch_b"][l], "w_pc": p["w_branch_c"][l],
        "w_o": p["w_o"][l], "ln1_g": row(p["ln1_g"]), "ln1_b": row(p["ln1_b"]),
        "layer": l, "w_fa": p["w_ff_a"], "w_fb": p["w_ff_b"], "cw": p["conv_w"][l],
        "cb": row(p["conv_b"]), "w_fd": p["w_ff_down"],
        "ln2_g": row(p["ln2_g"]), "ln2_b": row(p["ln2_b"]),
    }


def kernel(x_prompt, x_sample, c_prompt, c_sample, cache_mla_ckv, cache_mla_krope, cache_band_k, cache_band_v, state_ret, state_conv, w_ada, b_ada, w_in, g_q_lora, g_kv_lora, w_uq, w_ukv, rel_bias, g_ret_norm, w_branch_a, w_branch_b, w_branch_c, w_o, ln1_g, ln1_b, w_ff_a, w_ff_b, conv_w, conv_b, w_ff_down, ln2_g, ln2_b):
    nb_p, t_p, d = x_prompt.shape
    nb_s, t_s, _ = x_sample.shape
    depth = w_in.shape[0]
    past = cache_mla_ckv.shape[2]
    alpha = (2 * depth) ** 0.25

    nc = nb_p + nb_s
    ncp = -(-nc // 16) * 16
    c_all = jnp.pad(jnp.concatenate([c_prompt, c_sample], 0), ((0, ncp - nc), (0, 0)))
    tn_a = _tile(6 * d, 1024, LANE)
    ada = pl.pallas_call(
        _ada_kernel,
        grid=(depth, 6 * d // tn_a),
        in_specs=[pl.BlockSpec((ncp, d), lambda l, j: (0, 0)),
                  pl.BlockSpec((None, d, tn_a), lambda l, j: (l, 0, j)),
                  pl.BlockSpec((None, 1, tn_a), lambda l, j: (l, 0, j))],
        out_specs=pl.BlockSpec((None, ncp, tn_a), lambda l, j: (l, 0, j)),
        out_shape=jax.ShapeDtypeStruct((depth, ncp, 6 * d), _F32),
        compiler_params=_params("parallel", "arbitrary"),
        name="ada_ln",
    )(c_all, w_ada, b_ada[:, None, :])

    def mods(l, lo, n):
        a = ada[l, lo:lo + n]
        sh1, sc1, gt1, sh2, sc2, gt2 = [a[:, k * d:(k + 1) * d] for k in range(6)]
        return (sh1, 1.0 + sc1, 1.0 + gt1, sh2, 1.0 + sc2, 1.0 + gt2)

    bf = lambda a: a.astype(_BF)
    p = dict(g_q_lora=g_q_lora, g_kv_lora=g_kv_lora, rel_bias=rel_bias, g_ret_norm=g_ret_norm,
             w_branch_a=bf(w_branch_a), w_branch_b=bf(w_branch_b), w_branch_c=bf(w_branch_c),
             w_o=bf(w_o), ln1_g=ln1_g, ln1_b=ln1_b, w_ff_a=bf(w_ff_a), w_ff_b=bf(w_ff_b),
             conv_w=conv_w, conv_b=conv_b, w_ff_down=bf(w_ff_down), ln2_g=ln2_g, ln2_b=ln2_b)
    wi, wq, wkv = w_in, bf(w_uq), bf(w_ukv)
    krope_cache = jnp.pad(cache_mla_krope.reshape(-1, DR), ((0, 0), (0, LANE - DR)))

    streams = [(_Path(nb_p, t_p, 0, False), x_prompt, 0),
               (_Path(nb_s, t_s, past, True), x_sample, nb_p)]
    outs = []
    for path, x0, lo in streams:
        x = x0.reshape(path.rows, d)
        m0 = mods(0, lo, path.nb)
        h = _modulate(x, m0[1], m0[0], path)
        states = []
        for l in range(depth):
            m = mods(l, lo, path.nb)
            mn = mods(min(l + 1, depth - 1), lo, path.nb)
            cache = None
            if path.dense:
                cache = dict(layer=l, ckv=cache_mla_ckv, krope=krope_cache, band_k=cache_band_k,
                             band_v=cache_band_v, ret=state_ret[l], conv=state_conv[l])
            w = _layer_weights(l, wi, wq, wkv, None, p)
            x, h, st = _layer(x, h, path, m, (mn[1], mn[0]), cache, w, alpha)
            states.append(st)
        outs.append((x.reshape(path.nb, path.t, d), [jnp.stack(z) for z in zip(*states)]))
    (y_p, st_p), (y_s, st_s) = outs
    return (y_p, y_s, *st_p, *st_s)
```
